```python
import jax, jax.numpy as jnp
from jax import lax
import numpy as np

D_MODEL = 2048
BATCH = 2
SEQ = 4096
DEPTH = 1

CHUNK = 64
D_CONV = 1024
CONV_WIDTH = 31
N_HEADS = 16
HEAD_DIM = 64
D_ATTN = N_HEADS * HEAD_DIM
IDX_HEADS = 16
IDX_DIM = 64
TOPK_MAX = 256
Q_BLOCK = 128
N_BRANCH = 2
D_FF = -(-8 * D_MODEL // (3 * 256)) * 256
EPS = 1e-6

SZ_GLU = 2 * D_CONV
SZ_IQ = IDX_HEADS * IDX_DIM
SZ_GATE = N_BRANCH * D_MODEL
OFF_1 = SZ_GLU
OFF_2 = OFF_1 + D_ATTN
OFF_3 = OFF_2 + D_ATTN
OFF_4 = OFF_3 + D_ATTN
OFF_5 = OFF_4 + SZ_IQ
OFF_6 = OFF_5 + IDX_DIM
OFF_7 = OFF_6 + IDX_HEADS
D_IN = OFF_7 + SZ_GATE

kernel_name = "hybrid_conformer_dsa_block"


def rms_norm(x, g):
    xf = x.astype(jnp.float32)
    y = xf * lax.rsqrt(jnp.mean(xf * xf, axis=-1, keepdims=True) + EPS)
    return (y * g.astype(jnp.float32)).astype(x.dtype)


def layer_norm(x, g, b):
    xf = x.astype(jnp.float32)
    mu = jnp.mean(xf, axis=-1, keepdims=True)
    var = jnp.mean(jnp.square(xf - mu), axis=-1, keepdims=True)
    y = (xf - mu) * lax.rsqrt(var + EPS)
    return (y * g.astype(jnp.float32) + b.astype(jnp.float32)).astype(x.dtype)


def conv_branch(u_glu, w_dw, b_dw, g_ln, b_ln, w_out):
    a, gt = jnp.split(u_glu, 2, axis=-1)
    u = a * jax.nn.sigmoid(gt)
    u = lax.conv_general_dilated(
        u, w_dw[:, None, :], window_strides=(1,),
        padding=[(CONV_WIDTH - 1, 0)],
        dimension_numbers=("NWC", "WIO", "NWC"),
        feature_group_count=D_CONV) + b_dw
    u = jax.nn.silu(layer_norm(u, g_ln, b_ln))
    return u @ w_out


def sparse_attn_branch(q, k, v, q_idx, k_idx, w_idx, w_out):
    B, S = q.shape[0], q.shape[1]
    topk = min(TOPK_MAX, S // 4)
    nblk = S // Q_BLOCK
    qb_all = q.reshape(B, nblk, Q_BLOCK, N_HEADS, HEAD_DIM).transpose(1, 0, 2, 3, 4)
    qi_all = q_idx.reshape(B, nblk, Q_BLOCK, IDX_HEADS, IDX_DIM).transpose(1, 0, 2, 3, 4)
    wi_all = w_idx.reshape(B, nblk, Q_BLOCK, IDX_HEADS).transpose(1, 0, 2, 3)
    k4 = k.reshape(B, S, N_HEADS, HEAD_DIM)
    v4 = v.reshape(B, S, N_HEADS, HEAD_DIM)
    k_idx_f = k_idx.astype(jnp.float32)
    key_chunk = jnp.arange(S) // CHUNK
    idx_scale = (IDX_DIM ** -0.5) * (IDX_HEADS ** -0.5)
    attn_scale = HEAD_DIM ** -0.5
    gather = jax.vmap(lambda t, i: t[i])

    def block(args):
        blk, qb, qib, wb = args
        pos = blk * Q_BLOCK + jnp.arange(Q_BLOCK)
        q_chunk = pos // CHUNK
        admissible = key_chunk[None, :] <= q_chunk[:, None]
        rel = jax.nn.relu(jnp.einsum("bqhd,bsd->bqhs", qib.astype(jnp.float32), k_idx_f))
        score = jnp.einsum("bqhs,bqh->bqs", rel, wb.astype(jnp.float32)) * idx_scale
        score = jnp.where(admissible[None], score, -jnp.inf)
        _, sel = lax.top_k(score, topk)
        valid = key_chunk[sel] <= q_chunk[None, :, None]
        k_sel = gather(k4, sel)
        v_sel = gather(v4, sel)
        logits = jnp.einsum("bqhd,bqkhd->bqhk", qb, k_sel).astype(jnp.float32) * attn_scale
        logits = jnp.where(valid[:, :, None, :], logits, -jnp.inf)
        p = jax.nn.softmax(logits, axis=-1).astype(v4.dtype)
        return jnp.einsum("bqhk,bqkhd->bqhd", p, v_sel)

    out = lax.map(block, (jnp.arange(nblk), qb_all, qi_all, wi_all))
    out = out.transpose(1, 0, 2, 3, 4).reshape(B, S, D_ATTN)
    return out @ w_out


def setup_inputs(seed: int = 0) -> dict:
    key = jax.random.key(seed)
    ks = jax.random.split(key, 20)
    f32 = jnp.float32

    def nrm(k, shape, fan_in, gain=1.0):
        return jax.random.normal(k, shape, f32) * (gain * fan_in ** -0.5)

    def ones_noise(k, shape):
        return 1.0 + 0.05 * jax.random.normal(k, shape, f32)

    L = DEPTH
    return {
        "x": jax.random.normal(ks[0], (BATCH, SEQ, D_MODEL), f32),
        "c": jax.random.normal(ks[1], (BATCH, D_MODEL), f32),
        "w_ada": nrm(ks[2], (L, D_MODEL, 6 * D_MODEL), D_MODEL, 0.5),
        "b_ada": 0.02 * jax.random.normal(ks[3], (L, 6 * D_MODEL), f32),
        "g_pre_mix": ones_noise(ks[4], (L, D_MODEL)),
        "w_in": nrm(ks[5], (L, D_MODEL, D_IN), D_MODEL),
        "w_dw": nrm(ks[6], (L, CONV_WIDTH, D_CONV), CONV_WIDTH),
        "b_dw": 0.02 * jax.random.normal(ks[7], (L, D_CONV), f32),
        "g_conv_ln": ones_noise(ks[8], (L, D_CONV)),
        "b_conv_ln": 0.02 * jax.random.normal(ks[9], (L, D_CONV), f32),
        "w_conv_out": nrm(ks[10], (L, D_CONV, D_MODEL), D_CONV),
        "w_attn_out": nrm(ks[11], (L, D_ATTN, D_MODEL), D_ATTN),
        "w_o": nrm(ks[12], (L, D_MODEL, D_MODEL), D_MODEL),
        "g_post_mix": ones_noise(ks[13], (L, D_MODEL)),
        "g_pre_ffn": ones_noise(ks[14], (L, D_MODEL)),
        "w_gate": nrm(ks[15], (L, D_MODEL, D_FF), D_MODEL),
        "w_up": nrm(ks[16], (L, D_MODEL, D_FF), D_MODEL),
        "w_down": nrm(ks[17], (L, D_FF, D_MODEL), D_FF),
        "g_post_ffn": ones_noise(ks[18], (L, D_MODEL)),
    }


def reference(x, c, w_ada, b_ada, g_pre_mix, w_in, w_dw, b_dw, g_conv_ln, b_conv_ln,
              w_conv_out, w_attn_out, w_o, g_post_mix, g_pre_ffn, w_gate, w_up, w_down,
              g_post_ffn):
    c_act = jax.nn.silu(c)
    for l in range(DEPTH):
        mod = c_act @ w_ada[l] + b_ada[l]
        sh1, sc1, gt1, sh2, sc2, gt2 = jnp.split(mod[:, None, :], 6, axis=-1)

        h = rms_norm(x, g_pre_mix[l]) * (1.0 + sc1) + sh1
        proj = h @ w_in[l]
        u_glu, q, k, v, qi, ki, wi, gates = jnp.split(
            proj, [OFF_1, OFF_2, OFF_3, OFF_4, OFF_5, OFF_6, OFF_7], axis=-1)
        y_conv = conv_branch(u_glu, w_dw[l], b_dw[l], g_conv_ln[l], b_conv_ln[l], w_conv_out[l])
        y_attn = sparse_attn_branch(q, k, v, qi, ki, wi, w_attn_out[l])
        g = jax.nn.sigmoid(gates.astype(jnp.float32)).astype(x.dtype)
        g_conv, g_attn = jnp.split(g, 2, axis=-1)
        mixed = (g_conv * y_conv + g_attn * y_attn) @ w_o[l]
        x = x + gt1 * rms_norm(mixed, g_post_mix[l])

        h = rms_norm(x, g_pre_ffn[l]) * (1.0 + sc2) + sh2
        f = (jax.nn.silu(h @ w_gate[l]) * (h @ w_up[l])) @ w_down[l]
        x = x + gt2 * rms_norm(f, g_post_ffn[l])
    return x
```

```python
import functools

import jax
import jax.numpy as jnp
from jax import lax
from jax.experimental import pallas as pl
from jax.experimental.pallas import tpu as pltpu

F32 = jnp.float32
BF16 = jnp.bfloat16
I32 = jnp.int32

CHUNK = 64
CHUNK_SHIFT = 6
D_CONV = 1024
CONV_WIDTH = 31
N_HEADS = 16
HEAD_DIM = 64
D_ATTN = N_HEADS * HEAD_DIM
IDX_HEADS = 16
IDX_DIM = 64
TOPK_MAX = 256
EPS = 1e-6

LANES = 128
SUBLANES = 8
VMEM_LIMIT_BYTES = 56 * 1024 * 1024

NEG_BIG = -1e30
INT_MIN = -(2 ** 31)
NEG_INF_KEY = (0xFF800000 ^ 0x7FFFFFFF) - 2 ** 32


def _sigmoid(x):
    return 1.0 / (1.0 + jnp.exp(-x))


def _dot(a, b):
    return jnp.dot(a, b, preferred_element_type=F32)


def _dot_nt(a, b):
    return lax.dot_general(a, b, (((1,), (1,)), ((), ())), preferred_element_type=F32)


def _params(sem):
    return pltpu.CompilerParams(dimension_semantics=sem, vmem_limit_bytes=VMEM_LIMIT_BYTES)


def _ada_kernel(c_ref, w_ref, b_ref, o_ref):
    c = c_ref[...]
    ca = (c * _sigmoid(c)).astype(BF16)
    o_ref[...] = _dot(ca, w_ref[...].astype(BF16)) + b_ref[...]


def _ada(c8, w, b, tn=1024):
    rows, d = c8.shape
    n = w.shape[1]
    return pl.pallas_call(
        _ada_kernel,
        out_shape=jax.ShapeDtypeStruct((rows, n), F32),
        grid=(n // tn,),
        in_specs=[
            pl.BlockSpec((rows, d), lambda j: (0, 0)),
            pl.BlockSpec((d, tn), lambda j: (0, j)),
            pl.BlockSpec((1, tn), lambda j: (0, j)),
        ],
        out_specs=pl.BlockSpec((rows, tn), lambda j: (0, j)),
        compiler_params=_params(("arbitrary",)),
        name="ada",
    )(c8, w, b)


ROW_CHUNK = 128


def _adaln_rows(x, g, sc, sh):
    ms = jnp.mean(x * x, axis=-1, keepdims=True)
    y = (x * lax.rsqrt(ms + EPS)) * g
    return y * (1.0 + sc) + sh


def _proj_kernel(x_ref, g_ref, sc_ref, sh_ref, w_ref, ws_ref, o_ref, kk_ref, wi_ref, h_ref):
    j = pl.program_id(1)
    tm = x_ref.shape[0]

    @pl.when(j == 0)
    def _():
        g = g_ref[...]
        sc = sc_ref[0]
        sh = sh_ref[0]

        def body(r, carry):
            r0 = pl.multiple_of(r * ROW_CHUNK, ROW_CHUNK)
            h = _adaln_rows(x_ref[pl.ds(r0, ROW_CHUNK), :], g, sc, sh).astype(BF16)
            h_ref[pl.ds(r0, ROW_CHUNK), :] = h
            small = _dot(h, ws_ref[...])
            kk_ref[pl.ds(r0, ROW_CHUNK), :] = small[:, :LANES].astype(BF16)
            wi_ref[pl.ds(r0, ROW_CHUNK), :] = small[:, LANES:]
            return carry

        lax.fori_loop(0, tm // ROW_CHUNK, body, 0)

    o_ref[...] = _dot(h_ref[...], w_ref[...]).astype(BF16)


def _proj(x2, g, sc, sh, w_big, w_small, seq, tm=1024, tn=1024):
    m, d = x2.shape
    n = w_big.shape[1]
    per_b = seq // tm
    return pl.pallas_call(
        _proj_kernel,
        out_shape=(
            jax.ShapeDtypeStruct((m, n), BF16),
            jax.ShapeDtypeStruct((m, LANES), BF16),
            jax.ShapeDtypeStruct((m, LANES), F32),
        ),
        grid=(m // tm, n // tn),
        in_specs=[
            pl.BlockSpec((tm, d), lambda i, j: (i, 0)),
            pl.BlockSpec((1, d), lambda i, j: (0, 0)),
            pl.BlockSpec((1, 1, d), lambda i, j: (i // per_b, 0, 0)),
            pl.BlockSpec((1, 1, d), lambda i, j: (i // per_b, 0, 0)),
            pl.BlockSpec((d, tn), lambda i, j: (0, j)),
            pl.BlockSpec((d, 2 * LANES), lambda i, j: (0, 0)),
        ],
        out_specs=(
            pl.BlockSpec((tm, tn), lambda i, j: (i, j)),
            pl.BlockSpec((tm, LANES), lambda i, j: (i, 0)),
            pl.BlockSpec((tm, LANES), lambda i, j: (i, 0)),
        ),
        scratch_shapes=[pltpu.VMEM((tm, d), BF16)],
        compiler_params=_params(("arbitrary", "arbitrary")),
        name="proj",
    )(x2, g, sc, sh, w_big, w_small)


CONV_HALO = 32
CONV_ROWS = 64


def _conv_kernel(a_ref, gt_ref, w_ref, bdw_ref, gln_ref, bln_ref, o_ref, u_ref, cv_ref):
    s = pl.program_id(1)
    ts = a_ref.shape[0]
    dc = a_ref.shape[1]

    @pl.when(s == 0)
    def _():
        u_ref[0:CONV_HALO, :] = jnp.zeros((CONV_HALO, dc), F32)

    @pl.when(s > 0)
    def _():
        u_ref[0:CONV_HALO, :] = u_ref[ts:ts + CONV_HALO, :]

    def glu_body(r, carry):
        r0 = pl.multiple_of(r * CONV_ROWS, CONV_ROWS)
        a = a_ref[pl.ds(r0, CONV_ROWS), :].astype(F32)
        g = gt_ref[pl.ds(r0, CONV_ROWS), :].astype(F32)
        u_ref[pl.ds(CONV_HALO + r0, CONV_ROWS), :] = a * _sigmoid(g)
        return carry

    lax.fori_loop(0, ts // CONV_ROWS, glu_body, 0)

    base = CONV_HALO - (CONV_WIDTH - 1)

    def conv_body(r, carry):
        r0 = pl.multiple_of(r * CONV_ROWS, CONV_ROWS)
        for lc in range(dc // LANES):
            cols = slice(lc * LANES, (lc + 1) * LANES)
            win = u_ref[pl.ds(r0, CONV_ROWS + CONV_HALO), cols]
            w = w_ref[:, cols]
            acc = jnp.zeros((CONV_ROWS, LANES), F32)
            for b in range(SUBLANES):
                n_a = (CONV_WIDTH - 1 - b) // SUBLANES + 1
                shifted = win[base + b:base + b + CONV_ROWS + SUBLANES * (n_a - 1), :]
                for a in range(n_a):
                    jtap = SUBLANES * a + b
                    acc = acc + shifted[SUBLANES * a:SUBLANES * a + CONV_ROWS, :] * w[jtap:jtap + 1, :]
            cv_ref[pl.ds(r0, CONV_ROWS), cols] = acc + bdw_ref[:, cols]
        xr = cv_ref[pl.ds(r0, CONV_ROWS), :]
        mu = jnp.mean(xr, axis=-1, keepdims=True)
        xc = xr - mu
        var = jnp.mean(xc * xc, axis=-1, keepdims=True)
        y = (xc * lax.rsqrt(var + EPS)) * gln_ref[...] + bln_ref[...]
        o_ref[pl.ds(r0, CONV_ROWS), :] = (y * _sigmoid(y)).astype(BF16)
        return carry

    lax.fori_loop(0, ts // CONV_ROWS, conv_body, 0)


def _conv(p, w_dw, b_dw, g_ln, b_ln, batch, seq, ts=512):
    m = p.shape[0]
    dc = w_dw.shape[1]
    per_b = seq // ts
    return pl.pallas_call(
        _conv_kernel,
        out_shape=jax.ShapeDtypeStruct((m, dc), BF16),
        grid=(batch, per_b),
        in_specs=[
            pl.BlockSpec((ts, dc), lambda b, s: (b * per_b + s, 0)),
            pl.BlockSpec((ts, dc), lambda b, s: (b * per_b + s, 1)),
            pl.BlockSpec((CONV_WIDTH, dc), lambda b, s: (0, 0)),
            pl.BlockSpec((1, dc), lambda b, s: (0, 0)),
            pl.BlockSpec((1, dc), lambda b, s: (0, 0)),
            pl.BlockSpec((1, dc), lambda b, s: (0, 0)),
        ],
        out_specs=pl.BlockSpec((ts, dc), lambda b, s: (b * per_b + s, 0)),
        scratch_shapes=[
            pltpu.VMEM((ts + CONV_HALO, dc), F32),
            pltpu.VMEM((ts, dc), F32),
        ],
        compiler_params=_params(("arbitrary", "arbitrary")),
        name="conv",
    )(p, p, w_dw, b_dw, g_ln, b_ln)


Q_TILE = 128
KEY_TILE = 512
LANE_CHUNKS = KEY_TILE // LANES
INDEX_BITS = 13


def _fold_lanes(acc, t, op):
    for ch in range(LANE_CHUNKS):
        acc = op(acc, t[:, ch * LANES:(ch + 1) * LANES])
    return acc


def _attn_kernel(qi_ref, q_ref, k_ref, v_ref, kk_ref, wi_ref, o_ref,
                 qim_ref, qm_ref, key_ref, bias_ref, l_ref, cut_ref, *, topk):
    i = pl.program_id(1)
    p0 = i * Q_TILE
    n_kt = (p0 + Q_TILE + KEY_TILE - 1) // KEY_TILE
    idx_scale = (IDX_DIM ** -0.5) * (IDX_HEADS ** -0.5)
    attn_scale = HEAD_DIM ** -0.5

    lane = lax.broadcasted_iota(I32, (Q_TILE, LANES), 1)
    lo = lane < HEAD_DIM
    row = lax.broadcasted_iota(I32, (Q_TILE, KEY_TILE), 0)
    col = lax.broadcasted_iota(I32, (Q_TILE, KEY_TILE), 1)
    q_chunk = (p0 + row) >> CHUNK_SHIFT

    for hp in range(N_HEADS // 2):
        cols = slice(hp * LANES, (hp + 1) * LANES)
        qi_p = qi_ref[:, cols]
        q_p = q_ref[:, cols].astype(F32) * attn_scale
        zero_i = jnp.zeros_like(qi_p)
        qim_ref[2 * hp] = jnp.where(lo, qi_p, zero_i)
        qim_ref[2 * hp + 1] = jnp.where(lo, zero_i, qi_p)
        qm_ref[2 * hp] = jnp.where(lo, q_p, 0.0).astype(BF16)
        qm_ref[2 * hp + 1] = jnp.where(lo, 0.0, q_p).astype(BF16)

    wi = wi_ref[...]

    def score_body(kt, carry):
        k0 = pl.multiple_of(kt * KEY_TILE, KEY_TILE)
        kk_t = kk_ref[pl.ds(k0, KEY_TILE), :]
        acc = jnp.zeros((Q_TILE, KEY_TILE), F32)
        for h in range(IDX_HEADS):
            rel = _dot_nt(qim_ref[h], kk_t)
            acc = acc + jnp.maximum(rel, 0.0) * wi[:, h:h + 1]
        score = acc * idx_scale
        adm = ((k0 + col) >> CHUNK_SHIFT) <= q_chunk
        score = jnp.where(adm, score, -jnp.inf)
        score = jnp.where(score == 0.0, 0.0, score)
        bits = pltpu.bitcast(score, I32)
        key_ref[kt] = bits ^ ((bits >> 31) & 0x7FFFFFFF)
        return carry

    lax.fori_loop(0, n_kt, score_body, 0)

    def count(indicator):
        def body(kt, c):
            kt_keys = key_ref[kt]
            for ch in range(LANE_CHUNKS):
                kc = kt_keys[:, ch * LANES:(ch + 1) * LANES]
                c = c + indicator(kc, kt * KEY_TILE + ch * LANES + lane)
            return c

        c = lax.fori_loop(0, n_kt, body, jnp.zeros((Q_TILE, LANES), F32))
        return jnp.sum(c, axis=-1, keepdims=True)

    kf = float(topk)

    def bisect_body(it, prefix):
        bit = lax.shift_left(jnp.int32(1), jnp.int32(31) - it)
        cand_u = prefix | bit
        cand_s = cand_u ^ INT_MIN
        cnt = count(lambda kc, _: jnp.where(kc >= cand_s, 1.0, 0.0))
        return jnp.where(cnt >= kf, cand_u, prefix)

    prefix = lax.fori_loop(0, 32, bisect_body, jnp.zeros((Q_TILE, LANES), I32))
    thr = prefix ^ INT_MIN

    n_gt = count(lambda kc, _: jnp.where(kc > thr, 1.0, 0.0))
    n_ge = count(lambda kc, _: jnp.where(kc >= thr, 1.0, 0.0))
    want = kf - n_gt
    cut_ref[...] = jnp.full((Q_TILE, LANES), 2 ** INDEX_BITS, I32)
    tied = jnp.where(thr[:, :1] == NEG_INF_KEY, 0.0, jnp.where(n_ge > kf, 1.0, 0.0))
    surplus = jnp.max(tied)

    @pl.when(surplus > 0.0)
    def _():
        def cut_body(it, cut):
            bit = lax.shift_left(jnp.int32(1), jnp.int32(INDEX_BITS - 1) - it)
            cand = cut | bit
            cnt = count(lambda kc, idx: jnp.where(kc == thr, jnp.where(idx < cand, 1.0, 0.0), 0.0))
            return jnp.where(cnt <= want, cand, cut)

        cut_ref[...] = lax.fori_loop(0, INDEX_BITS, cut_body, jnp.zeros((Q_TILE, LANES), I32))

    cut = cut_ref[...]

    def bias_body(kt, carry):
        k0 = pl.multiple_of(kt * KEY_TILE, KEY_TILE)
        kt_keys = key_ref[kt]
        adm = ((k0 + col) >> CHUNK_SHIFT) <= q_chunk
        pieces = []
        for ch in range(LANE_CHUNKS):
            kc = kt_keys[:, ch * LANES:(ch + 1) * LANES]
            idx = k0 + ch * LANES + lane
            tie = jnp.where(kc == thr, jnp.where(idx < cut, 0.0, NEG_BIG), NEG_BIG)
            pieces.append(jnp.where(kc > thr, 0.0, tie))
        b = jnp.concatenate(pieces, axis=-1)
        bias_ref[kt] = jnp.where(adm, b, NEG_BIG)
        return carry

    lax.fori_loop(0, n_kt, bias_body, 0)

    for hp in range(N_HEADS // 2):
        cols = slice(hp * LANES, (hp + 1) * LANES)
        outs = []
        for par in range(2):
            h = 2 * hp + par
            qh = qm_ref[h]

            def logit_body(kt, mx):
                k0 = pl.multiple_of(kt * KEY_TILE, KEY_TILE)
                s = _dot_nt(qh, k_ref[pl.ds(k0, KEY_TILE), cols]) + bias_ref[kt]
                l_ref[kt] = s
                return _fold_lanes(mx, s, jnp.maximum)

            mx = lax.fori_loop(0, n_kt, logit_body, jnp.full((Q_TILE, LANES), NEG_BIG, F32))
            m = jnp.max(mx, axis=-1, keepdims=True)

            def pv_body(kt, carry):
                ssum, acc = carry
                k0 = pl.multiple_of(kt * KEY_TILE, KEY_TILE)
                e = jnp.exp(l_ref[kt] - m)
                ssum = _fold_lanes(ssum, e, jnp.add)
                acc = acc + _dot(e.astype(BF16), v_ref[pl.ds(k0, KEY_TILE), cols])
                return ssum, acc

            zeros = jnp.zeros((Q_TILE, LANES), F32)
            ssum, acc = lax.fori_loop(0, n_kt, pv_body, (zeros, zeros))
            outs.append(acc / jnp.sum(ssum, axis=-1, keepdims=True))
        o_ref[:, cols] = jnp.where(lo, outs[0], outs[1]).astype(BF16)


def _attn(p, kk, wi, batch, seq, topk):
    m = p.shape[0]
    nq = seq // Q_TILE
    n_kt = seq // KEY_TILE
    once = pl.Buffered(1)
    return pl.pallas_call(
        functools.partial(_attn_kernel, topk=topk),
        out_shape=jax.ShapeDtypeStruct((m, D_ATTN), BF16),
        grid=(batch, nq),
        in_specs=[
            pl.BlockSpec((Q_TILE, D_ATTN), lambda b, i: (b * nq + i, 5)),
            pl.BlockSpec((Q_TILE, D_ATTN), lambda b, i: (b * nq + i, 2)),
            pl.BlockSpec((seq, D_ATTN), lambda b, i: (b, 3), pipeline_mode=once),
            pl.BlockSpec((seq, D_ATTN), lambda b, i: (b, 4), pipeline_mode=once),
            pl.BlockSpec((seq, LANES), lambda b, i: (b, 0), pipeline_mode=once),
            pl.BlockSpec((Q_TILE, LANES), lambda b, i: (b * nq + i, 0)),
        ],
        out_specs=pl.BlockSpec((Q_TILE, D_ATTN), lambda b, i: (b * nq + i, 0)),
        scratch_shapes=[
            pltpu.VMEM((IDX_HEADS, Q_TILE, LANES), BF16),
            pltpu.VMEM((N_HEADS, Q_TILE, LANES), BF16),
            pltpu.VMEM((n_kt, Q_TILE, KEY_TILE), I32),
            pltpu.VMEM((n_kt, Q_TILE, KEY_TILE), F32),
            pltpu.VMEM((n_kt, Q_TILE, KEY_TILE), F32),
            pltpu.VMEM((Q_TILE, LANES), I32),
        ],
        compiler_params=_params(("arbitrary", "arbitrary")),
        name="attn",
    )(p, p, p, p, kk, wi)


MERGE_ROWS = 256


def _rms_rows(x, g):
    ms = jnp.mean(x * x, axis=-1, keepdims=True)
    return (x * lax.rsqrt(ms + EPS)) * g


def _merge_kernel(uc_ref, oa_ref, gc_ref, ga_ref, x_ref, gt_ref, gp_ref,
                  wc_ref, wa_ref, wo_ref, o_ref):
    tm = x_ref.shape[0]
    gt = gt_ref[0]
    gp = gp_ref[...]

    def body(r, carry):
        r0 = pl.multiple_of(r * MERGE_ROWS, MERGE_ROWS)
        rows = pl.ds(r0, MERGE_ROWS)
        yc = _dot(uc_ref[rows, :], wc_ref[...])
        ya = _dot(oa_ref[rows, :], wa_ref[...])
        gc = _sigmoid(gc_ref[rows, :].astype(F32))
        ga = _sigmoid(ga_ref[rows, :].astype(F32))
        merged = (gc * yc + ga * ya).astype(BF16)
        mixed = _dot(merged, wo_ref[...])
        o_ref[rows, :] = x_ref[rows, :] + gt * _rms_rows(mixed, gp)
        return carry

    lax.fori_loop(0, tm // MERGE_ROWS, body, 0)


def _merge(uc, oa, p, x2, gt1, g_post, wc, wa, wo, seq, tm=256):
    m, d = x2.shape
    dc = uc.shape[1]
    per_b = seq // tm
    gate_blk = p.shape[1] // d - 2
    once = pl.Buffered(1)
    return pl.pallas_call(
        _merge_kernel,
        out_shape=jax.ShapeDtypeStruct((m, d), F32),
        grid=(m // tm,),
        in_specs=[
            pl.BlockSpec((tm, dc), lambda i: (i, 0)),
            pl.BlockSpec((tm, dc), lambda i: (i, 0)),
            pl.BlockSpec((tm, d), lambda i: (i, gate_blk)),
            pl.BlockSpec((tm, d), lambda i: (i, gate_blk + 1)),
            pl.BlockSpec((tm, d), lambda i: (i, 0)),
            pl.BlockSpec((1, 1, d), lambda i: (i // per_b, 0, 0)),
            pl.BlockSpec((1, d), lambda i: (0, 0)),
            pl.BlockSpec((dc, d), lambda i: (0, 0), pipeline_mode=once),
            pl.BlockSpec((dc, d), lambda i: (0, 0), pipeline_mode=once),
            pl.BlockSpec((d, d), lambda i: (0, 0), pipeline_mode=once),
        ],
        out_specs=pl.BlockSpec((tm, d), lambda i: (i, 0)),
        compiler_params=_params(("arbitrary",)),
        name="merge",
    )(uc, oa, p, p, x2, gt1, g_post, wc, wa, wo)


FFN_ROWS = 256


def _ffn_kernel(x_ref, g_ref, sc_ref, sh_ref, gt_ref, gp_ref, wg_ref, wu_ref, wd_ref,
                o_ref, h_ref, acc_ref):
    j = pl.program_id(1)
    tm = x_ref.shape[0]
    n_rows = tm // FFN_ROWS

    @pl.when(j == 0)
    def _():
        g = g_ref[...]
        sc = sc_ref[0]
        sh = sh_ref[0]

        def body(r, carry):
            rows = pl.ds(pl.multiple_of(r * ROW_CHUNK, ROW_CHUNK), ROW_CHUNK)
            h_ref[rows, :] = _adaln_rows(x_ref[rows, :], g, sc, sh).astype(BF16)
            return carry

        lax.fori_loop(0, tm // ROW_CHUNK, body, 0)

    def body(r, carry):
        rows = pl.ds(pl.multiple_of(r * FFN_ROWS, FFN_ROWS), FFN_ROWS)
        h = h_ref[rows, :]
        a = _dot(h, wg_ref[...])
        b = _dot(h, wu_ref[...])
        act = ((a * _sigmoid(a)) * b).astype(BF16)
        part = _dot(act, wd_ref[...])

        @pl.when(j == 0)
        def _():
            acc_ref[rows, :] = part

        @pl.when(j > 0)
        def _():
            acc_ref[rows, :] = acc_ref[rows, :] + part

        return carry

    lax.fori_loop(0, n_rows, body, 0)

    @pl.when(j == pl.num_programs(1) - 1)
    def _():
        gt = gt_ref[0]
        gp = gp_ref[...]

        def body(r, carry):
            rows = pl.ds(pl.multiple_of(r * ROW_CHUNK, ROW_CHUNK), ROW_CHUNK)
            o_ref[rows, :] = x_ref[rows, :] + gt * _rms_rows(acc_ref[rows, :], gp)
            return carry

        lax.fori_loop(0, tm // ROW_CHUNK, body, 0)


def _ffn(x2, g_pre, sc2, sh2, gt2, g_post, wg, wu, wd, seq, tm=512, tf=512):
    m, d = x2.shape
    dff = wg.shape[1]
    per_b = seq // tm
    return pl.pallas_call(
        _ffn_kernel,
        out_shape=jax.ShapeDtypeStruct((m, d), F32),
        grid=(m // tm, dff // tf),
        in_specs=[
            pl.BlockSpec((tm, d), lambda i, j: (i, 0)),
            pl.BlockSpec((1, d), lambda i, j: (0, 0)),
            pl.BlockSpec((1, 1, d), lambda i, j: (i // per_b, 0, 0)),
            pl.BlockSpec((1, 1, d), lambda i, j: (i // per_b, 0, 0)),
            pl.BlockSpec((1, 1, d), lambda i, j: (i // per_b, 0, 0)),
            pl.BlockSpec((1, d), lambda i, j: (0, 0)),
            pl.BlockSpec((d, tf), lambda i, j: (0, j)),
            pl.BlockSpec((d, tf), lambda i, j: (0, j)),
            pl.BlockSpec((tf, d), lambda i, j: (j, 0)),
        ],
        out_specs=pl.BlockSpec((tm, d), lambda i, j: (i, 0)),
        scratch_shapes=[pltpu.VMEM((tm, d), BF16), pltpu.VMEM((tm, d), F32)],
        compiler_params=_params(("arbitrary", "arbitrary")),
        name="ffn",
    )(x2, g_pre, sc2, sh2, gt2, g_post, wg, wu, wd)


def kernel(x, c, w_ada, b_ada, g_pre_mix, w_in, w_dw, b_dw, g_conv_ln, b_conv_ln, w_conv_out, w_attn_out, w_o, g_post_mix, g_pre_ffn, w_gate, w_up, w_down, g_post_ffn):
    batch, seq, d = x.shape
    depth = w_ada.shape[0]
    topk = min(TOPK_MAX, seq // 4)

    off_glu = 2 * D_CONV
    off_ki = off_glu + 3 * D_ATTN + IDX_HEADS * IDX_DIM
    off_wi = off_ki + IDX_DIM
    off_gate = off_wi + IDX_HEADS

    c8 = jnp.zeros((SUBLANES, d), F32).at[:batch].set(c)
    x2 = x.reshape(batch * seq, d)
    for l in range(depth):
        mod = _ada(c8, w_ada[l], b_ada[l][None, :])[:batch]
        sh1, sc1, gt1, sh2, sc2, gt2 = [t[:, None, :] for t in jnp.split(mod, 6, axis=-1)]

        w = w_in[l]
        w_big = jnp.concatenate([w[:, :off_ki], w[:, off_gate:]], axis=1).astype(BF16)
        w_ki = w[:, off_ki:off_wi]
        w_wi = jnp.pad(w[:, off_wi:off_gate], ((0, 0), (0, LANES - IDX_HEADS)))
        w_small = jnp.concatenate([w_ki, w_ki, w_wi], axis=1).astype(BF16)

        p, kk, wi = _proj(x2, g_pre_mix[l][None, :], sc1, sh1, w_big, w_small, seq)
        uc = _conv(p, w_dw[l], b_dw[l][None, :], g_conv_ln[l][None, :], b_conv_ln[l][None, :],
                   batch, seq)
        oa = _attn(p, kk, wi, batch, seq, topk)
        x2 = _merge(uc, oa, p, x2, gt1, g_post_mix[l][None, :],
                    w_conv_out[l].astype(BF16), w_attn_out[l].astype(BF16), w_o[l].astype(BF16), seq)
        x2 = _ffn(x2, g_pre_ffn[l][None, :], sc2, sh2, gt2, g_post_ffn[l][None, :],
                  w_gate[l].astype(BF16), w_up[l].astype(BF16), w_down[l].astype(BF16), seq)
    return x2.reshape(batch, seq, d)
```

```python
import functools

import jax
import jax.numpy as jnp
from jax import lax
from jax.experimental import pallas as pl
from jax.experimental.pallas import tpu as pltpu

F32 = jnp.float32
BF16 = jnp.bfloat16
I32 = jnp.int32

CHUNK = 64
CHUNK_SHIFT = 6
D_CONV = 1024
CONV_WIDTH = 31
N_HEADS = 16
HEAD_DIM = 64
D_ATTN = N_HEADS * HEAD_DIM
IDX_HEADS = 16
IDX_DIM = 64
TOPK_MAX = 256
EPS = 1e-6

LANES = 128
SUBLANES = 8
VMEM_LIMIT_BYTES = 56 * 1024 * 1024

LOG2_E = 1.4426950408889634
NEG_BIG = -1e30
INT_MIN = -(2 ** 31)
NEG_INF_KEY = (0xFF800000 ^ 0x7FFFFFFF) - 2 ** 32


def _sigmoid(x):
    return 1.0 / (1.0 + jnp.exp(-x))


def _dot(a, b):
    return jnp.dot(a, b, preferred_element_type=F32)


def _dot_nt(a, b):
    return lax.dot_general(a, b, (((1,), (1,)), ((), ())), preferred_element_type=F32)


def _params(sem):
    return pltpu.CompilerParams(dimension_semantics=sem, vmem_limit_bytes=VMEM_LIMIT_BYTES)


def _ada_kernel(c_ref, w_ref, b_ref, o_ref):
    c = c_ref[...]
    ca = (c * _sigmoid(c)).astype(BF16)
    o_ref[...] = _dot(ca, w_ref[...].astype(BF16)) + b_ref[...]


def _ada(c8, w, b, tn=1024):
    rows, d = c8.shape
    n = w.shape[1]
    return pl.pallas_call(
        _ada_kernel,
        out_shape=jax.ShapeDtypeStruct((rows, n), F32),
        grid=(n // tn,),
        in_specs=[
            pl.BlockSpec((rows, d), lambda j: (0, 0)),
            pl.BlockSpec((d, tn), lambda j: (0, j)),
            pl.BlockSpec((1, tn), lambda j: (0, j)),
        ],
        out_specs=pl.BlockSpec((rows, tn), lambda j: (0, j)),
        compiler_params=_params(("arbitrary",)),
        name="ada",
    )(c8, w, b)


ROW_CHUNK = 128


def _adaln_rows(x, g, sc, sh):
    ms = jnp.mean(x * x, axis=-1, keepdims=True)
    y = (x * lax.rsqrt(ms + EPS)) * g
    return y * (1.0 + sc) + sh


def _proj_kernel(x_ref, g_ref, sc_ref, sh_ref, w_ref, wg_ref, ws_ref, o_ref, kk_ref, wi_ref, h_ref,
                 *, n_main):
    j = pl.program_id(1)
    tm = x_ref.shape[0]

    @pl.when(j == 0)
    def _():
        g = g_ref[...]
        sc = sc_ref[0]
        sh = sh_ref[0]

        def body(r, carry):
            r0 = pl.multiple_of(r * ROW_CHUNK, ROW_CHUNK)
            h = _adaln_rows(x_ref[pl.ds(r0, ROW_CHUNK), :], g, sc, sh).astype(BF16)
            h_ref[pl.ds(r0, ROW_CHUNK), :] = h
            small = _dot(h, ws_ref[...])
            kk_ref[pl.ds(r0, ROW_CHUNK), :] = small[:, :LANES].astype(BF16)
            wi_ref[pl.ds(r0, ROW_CHUNK), :] = small[:, LANES:]
            return carry

        lax.fori_loop(0, tm // ROW_CHUNK, body, 0)

    @pl.when(j < n_main)
    def _():
        o_ref[...] = _dot(h_ref[...], w_ref[...]).astype(BF16)

    @pl.when(j >= n_main)
    def _():
        o_ref[...] = _dot(h_ref[...], wg_ref[...]).astype(BF16)


def _proj(x2, g, sc, sh, w_main, w_gates, w_small, seq, tm=1024, tn=1024):
    m, d = x2.shape
    n_main = w_main.shape[1] // tn
    n = w_main.shape[1] + w_gates.shape[1]
    per_b = seq // tm
    return pl.pallas_call(
        functools.partial(_proj_kernel, n_main=n_main),
        out_shape=(
            jax.ShapeDtypeStruct((m, n), BF16),
            jax.ShapeDtypeStruct((m, LANES), BF16),
            jax.ShapeDtypeStruct((m, LANES), F32),
        ),
        grid=(m // tm, n // tn),
        in_specs=[
            pl.BlockSpec((tm, d), lambda i, j: (i, 0)),
            pl.BlockSpec((1, d), lambda i, j: (0, 0)),
            pl.BlockSpec((1, 1, d), lambda i, j: (i // per_b, 0, 0)),
            pl.BlockSpec((1, 1, d), lambda i, j: (i // per_b, 0, 0)),
            pl.BlockSpec((d, tn), lambda i, j: (0, jnp.minimum(j, n_main - 1))),
            pl.BlockSpec((d, tn), lambda i, j: (0, jnp.maximum(j - n_main, 0))),
            pl.BlockSpec((d, 2 * LANES), lambda i, j: (0, 0)),
        ],
        out_specs=(
            pl.BlockSpec((tm, tn), lambda i, j: (i, j)),
            pl.BlockSpec((tm, LANES), lambda i, j: (i, 0)),
            pl.BlockSpec((tm, LANES), lambda i, j: (i, 0)),
        ),
        scratch_shapes=[pltpu.VMEM((tm, d), BF16)],
        compiler_params=_params(("arbitrary", "arbitrary")),
        name="proj",
    )(x2, g, sc, sh, w_main, w_gates, w_small)


CONV_HALO = 32
CONV_ROWS = 64


def _conv_kernel(a_ref, gt_ref, w_ref, bdw_ref, gln_ref, bln_ref, o_ref, u_ref, cv_ref):
    s = pl.program_id(1)
    ts = a_ref.shape[0]
    dc = a_ref.shape[1]

    @pl.when(s == 0)
    def _():
        u_ref[0:CONV_HALO, :] = jnp.zeros((CONV_HALO, dc), F32)

    @pl.when(s > 0)
    def _():
        u_ref[0:CONV_HALO, :] = u_ref[ts:ts + CONV_HALO, :]

    def glu_body(r, carry):
        r0 = pl.multiple_of(r * CONV_ROWS, CONV_ROWS)
        a = a_ref[pl.ds(r0, CONV_ROWS), :].astype(F32)
        g = gt_ref[pl.ds(r0, CONV_ROWS), :].astype(F32)
        u_ref[pl.ds(CONV_HALO + r0, CONV_ROWS), :] = a * _sigmoid(g)
        return carry

    lax.fori_loop(0, ts // CONV_ROWS, glu_body, 0)

    base = CONV_HALO - (CONV_WIDTH - 1)

    def conv_body(r, carry):
        r0 = pl.multiple_of(r * CONV_ROWS, CONV_ROWS)
        for lc in range(dc // LANES):
            cols = slice(lc * LANES, (lc + 1) * LANES)
            win = u_ref[pl.ds(r0, CONV_ROWS + CONV_HALO), cols]
            w = w_ref[:, cols]
            acc = jnp.zeros((CONV_ROWS, LANES), F32)
            for b in range(SUBLANES):
                n_a = (CONV_WIDTH - 1 - b) // SUBLANES + 1
                shifted = win[base + b:base + b + CONV_ROWS + SUBLANES * (n_a - 1), :]
                for a in range(n_a):
                    jtap = SUBLANES * a + b
                    acc = acc + shifted[SUBLANES * a:SUBLANES * a + CONV_ROWS, :] * w[jtap:jtap + 1, :]
            cv_ref[pl.ds(r0, CONV_ROWS), cols] = acc + bdw_ref[:, cols]
        xr = cv_ref[pl.ds(r0, CONV_ROWS), :]
        mu = jnp.mean(xr, axis=-1, keepdims=True)
        xc = xr - mu
        var = jnp.mean(xc * xc, axis=-1, keepdims=True)
        y = (xc * lax.rsqrt(var + EPS)) * gln_ref[...] + bln_ref[...]
        o_ref[pl.ds(r0, CONV_ROWS), :] = (y * _sigmoid(y)).astype(BF16)
        return carry

    lax.fori_loop(0, ts // CONV_ROWS, conv_body, 0)


def _conv(p, w_dw, b_dw, g_ln, b_ln, batch, seq, ts=512):
    m = p.shape[0]
    dc = w_dw.shape[1]
    per_b = seq // ts
    return pl.pallas_call(
        _conv_kernel,
        out_shape=jax.ShapeDtypeStruct((m, dc), BF16),
        grid=(batch, per_b),
        in_specs=[
            pl.BlockSpec((ts, dc), lambda b, s: (b * per_b + s, 0)),
            pl.BlockSpec((ts, dc), lambda b, s: (b * per_b + s, 1)),
            pl.BlockSpec((CONV_WIDTH, dc), lambda b, s: (0, 0)),
            pl.BlockSpec((1, dc), lambda b, s: (0, 0)),
            pl.BlockSpec((1, dc), lambda b, s: (0, 0)),
            pl.BlockSpec((1, dc), lambda b, s: (0, 0)),
        ],
        out_specs=pl.BlockSpec((ts, dc), lambda b, s: (b * per_b + s, 0)),
        scratch_shapes=[
            pltpu.VMEM((ts + CONV_HALO, dc), F32),
            pltpu.VMEM((ts, dc), F32),
        ],
        compiler_params=_params(("arbitrary", "arbitrary")),
        name="conv",
    )(p, p, w_dw, b_dw, g_ln, b_ln)


Q_TILE = 256
Q_SUB = 128
Q_SUBS = Q_TILE // Q_SUB
KEY_TILE = 256
KEY_SUB = 256
SEL_ROWS = 64
LANE_CHUNKS = KEY_TILE // LANES
INDEX_BITS = 13


def _fold_lanes(acc, t, op):
    for ch in range(t.shape[1] // LANES):
        acc = op(acc, t[:, ch * LANES:(ch + 1) * LANES])
    return acc


def _attn_kernel(qi_ref, q_ref, k_ref, v_ref, kk_ref, wi_ref, o_ref,
                 qim_ref, qm_ref, wrep_ref, key_ref, bias_ref, m_ref, s_ref, acc_ref, thr_ref,
                 cut_ref, cand_ref, idx_ref, *, topk):
    i = pl.program_id(1)
    p0 = i * Q_TILE
    n_kt = (p0 + Q_TILE + KEY_TILE - 1) // KEY_TILE
    idx_scale = (IDX_DIM ** -0.5) * (IDX_HEADS ** -0.5)
    attn_scale = HEAD_DIM ** -0.5

    lane = lax.broadcasted_iota(I32, (Q_TILE, LANES), 1)
    lo = lane < HEAD_DIM
    lo_sub = lax.broadcasted_iota(I32, (Q_SUB, LANES), 1) < HEAD_DIM
    row = lax.broadcasted_iota(I32, (Q_TILE, KEY_TILE), 0)
    col = lax.broadcasted_iota(I32, (Q_TILE, KEY_TILE), 1)
    q_chunk = (p0 + row) >> CHUNK_SHIFT

    wi = wi_ref[...]
    for hp in range(N_HEADS // 2):
        cols = slice(hp * LANES, (hp + 1) * LANES)
        q_p = q_ref[:, cols].astype(F32) * (attn_scale * LOG2_E)
        qm_ref[2 * hp] = jnp.where(lo, q_p, 0.0).astype(BF16)
        qm_ref[2 * hp + 1] = jnp.where(lo, 0.0, q_p).astype(BF16)
        for rs in range(Q_SUBS):
            qi_p = qi_ref[rs * Q_SUB:(rs + 1) * Q_SUB, cols]
            zero_i = jnp.zeros_like(qi_p)
            qim_ref[rs, (2 * hp) * Q_SUB:(2 * hp + 1) * Q_SUB, :] = jnp.where(lo_sub, qi_p, zero_i)
            qim_ref[rs, (2 * hp + 1) * Q_SUB:(2 * hp + 2) * Q_SUB, :] = jnp.where(lo_sub, zero_i, qi_p)
    for h in range(IDX_HEADS):
        wrep_ref[h] = jnp.broadcast_to(wi[:, h:h + 1], (Q_TILE, LANES))

    sub_row = lax.broadcasted_iota(I32, (Q_SUB, KEY_SUB), 0)
    sub_col = lax.broadcasted_iota(I32, (Q_SUB, KEY_SUB), 1)

    def score_body(kt, carry):
        k0 = pl.multiple_of(kt * KEY_TILE, KEY_TILE)
        for ks in range(KEY_TILE // KEY_SUB):
            kk_s = kk_ref[pl.ds(k0 + ks * KEY_SUB, KEY_SUB), :]
            for rs in range(Q_SUBS):
                rows = slice(rs * Q_SUB, (rs + 1) * Q_SUB)
                rel = _dot_nt(qim_ref[rs], kk_s)
                acc = [jnp.zeros((Q_SUB, LANES), F32) for _ in range(KEY_SUB // LANES)]
                for h in range(IDX_HEADS):
                    w_h = wrep_ref[h, rows, :]
                    for ch in range(KEY_SUB // LANES):
                        r = rel[h * Q_SUB:(h + 1) * Q_SUB, ch * LANES:(ch + 1) * LANES]
                        acc[ch] = acc[ch] + jnp.maximum(r, 0.0) * w_h
                score = jnp.concatenate(acc, axis=-1) * idx_scale
                adm = ((k0 + ks * KEY_SUB + sub_col) >> CHUNK_SHIFT) <= (
                    (p0 + rs * Q_SUB + sub_row) >> CHUNK_SHIFT)
                key_ref[kt, rows, ks * KEY_SUB:(ks + 1) * KEY_SUB] = jnp.where(adm, score, -jnp.inf)
        return carry

    lax.fori_loop(0, n_kt, score_body, 0)

    kf = float(topk)

    def key_to_float(key_s):
        bits = key_s ^ ((key_s >> 31) & 0x7FFFFFFF)
        return jnp.where(key_s <= NEG_INF_KEY, -jnp.inf, pltpu.bitcast(bits, F32))

    lane_s = lax.broadcasted_iota(I32, (SEL_ROWS, LANES), 1)

    row_groups = [slice(rg * SEL_ROWS, (rg + 1) * SEL_ROWS) for rg in range(Q_TILE // SEL_ROWS)]

    def count(indicator):
        def body(kt, cs):
            out = []
            for rows, c in zip(row_groups, cs):
                for ch in range(LANE_CHUNKS):
                    kc = key_ref[kt, rows, ch * LANES:(ch + 1) * LANES]
                    c = c + indicator(kc, kt * KEY_TILE + ch * LANES + lane_s, rows)
                out.append(c)
            return tuple(out)

        zero = jnp.zeros((SEL_ROWS, LANES), F32)
        cs = lax.fori_loop(0, n_kt, body, tuple(zero for _ in row_groups))
        return jnp.sum(jnp.concatenate(cs, axis=0), axis=-1, keepdims=True)

    def bisect_body(it, prefix):
        bit = lax.shift_left(jnp.int32(1), jnp.int32(31) - it)
        cand_u = prefix | bit
        cand_ref[...] = key_to_float(cand_u ^ INT_MIN)
        cnt = count(lambda kc, _, rows: jnp.where(kc >= cand_ref[rows, :], 1.0, 0.0))
        return jnp.where(cnt >= kf, cand_u, prefix)

    prefix = lax.fori_loop(0, 32, bisect_body, jnp.zeros((Q_TILE, LANES), I32))
    thr = key_to_float(prefix ^ INT_MIN)
    thr_ref[...] = thr

    n_gt = count(lambda kc, _, rows: jnp.where(kc > thr_ref[rows, :], 1.0, 0.0))
    n_ge = count(lambda kc, _, rows: jnp.where(kc >= thr_ref[rows, :], 1.0, 0.0))
    want = kf - n_gt
    cut_ref[...] = jnp.full((Q_TILE, LANES), 2 ** INDEX_BITS, I32)
    tied = jnp.where(thr[:, :1] == -jnp.inf, 0.0, jnp.where(n_ge > kf, 1.0, 0.0))
    surplus = jnp.max(tied)

    @pl.when(surplus > 0.0)
    def _():
        def cut_body(it, cut):
            bit = lax.shift_left(jnp.int32(1), jnp.int32(INDEX_BITS - 1) - it)
            cand = cut | bit
            idx_ref[...] = cand
            cnt = count(lambda kc, idx, rows: jnp.where(
                kc == thr_ref[rows, :], jnp.where(idx < idx_ref[rows, :], 1.0, 0.0), 0.0))
            return jnp.where(cnt <= want, cand, cut)

        cut_ref[...] = lax.fori_loop(0, INDEX_BITS, cut_body, jnp.zeros((Q_TILE, LANES), I32))

    cut = cut_ref[...]

    def bias_body(kt, carry):
        k0 = pl.multiple_of(kt * KEY_TILE, KEY_TILE)
        kt_keys = key_ref[kt]
        adm = ((k0 + col) >> CHUNK_SHIFT) <= q_chunk
        pieces = []
        for ch in range(LANE_CHUNKS):
            kc = kt_keys[:, ch * LANES:(ch + 1) * LANES]
            idx = k0 + ch * LANES + lane
            tie = jnp.where(kc == thr, jnp.where(idx < cut, 0.0, -jnp.inf), -jnp.inf)
            pieces.append(jnp.where(kc > thr, 0.0, tie))
        b = jnp.concatenate(pieces, axis=-1)
        bias_ref[kt] = jnp.where(adm, b, -jnp.inf)
        return carry

    lax.fori_loop(0, n_kt, bias_body, 0)

    for h in range(N_HEADS):
        m_ref[h] = jnp.full((Q_TILE, LANES), NEG_BIG, F32)
        s_ref[h] = jnp.zeros((Q_TILE, LANES), F32)
        acc_ref[h] = jnp.zeros((Q_TILE, LANES), F32)

    def attn_body(kt, carry):
        k0 = pl.multiple_of(kt * KEY_TILE, KEY_TILE)
        for hp in range(N_HEADS // 2):
            cols = slice(hp * LANES, (hp + 1) * LANES)
            k_t = k_ref[pl.ds(k0, KEY_TILE), cols]
            v_t = v_ref[pl.ds(k0, KEY_TILE), cols]
            for h in (2 * hp, 2 * hp + 1):
                s = _dot_nt(qm_ref[h], k_t) + bias_ref[kt]
                t_max = jnp.max(_fold_lanes(jnp.full((Q_TILE, LANES), -jnp.inf, F32), s, jnp.maximum),
                                axis=-1, keepdims=True)
                m_old = m_ref[h]
                m_new = jnp.maximum(m_old, t_max)
                alpha = jnp.exp2(m_old - m_new)
                e = jnp.concatenate(
                    [jnp.exp2(s[:, ch * LANES:(ch + 1) * LANES] - m_new) for ch in range(LANE_CHUNKS)],
                    axis=-1)
                s_ref[h] = alpha * s_ref[h] + _fold_lanes(jnp.zeros((Q_TILE, LANES), F32), e, jnp.add)
                acc_ref[h] = alpha * acc_ref[h] + _dot(e.astype(BF16), v_t)
                m_ref[h] = m_new
        return carry

    lax.fori_loop(0, n_kt, attn_body, 0)

    for hp in range(N_HEADS // 2):
        outs = []
        for h in (2 * hp, 2 * hp + 1):
            outs.append(acc_ref[h] / jnp.sum(s_ref[h], axis=-1, keepdims=True))
        o_ref[:, hp * LANES:(hp + 1) * LANES] = jnp.where(lo, outs[0], outs[1]).astype(BF16)


def _attn(p, kk, wi, batch, seq, topk):
    m = p.shape[0]
    nq = seq // Q_TILE
    n_kt = seq // KEY_TILE
    once = pl.Buffered(1)
    return pl.pallas_call(
        functools.partial(_attn_kernel, topk=topk),
        out_shape=jax.ShapeDtypeStruct((m, D_ATTN), BF16),
        grid=(batch, nq),
        in_specs=[
            pl.BlockSpec((Q_TILE, D_ATTN), lambda b, i: (b * nq + i, 5)),
            pl.BlockSpec((Q_TILE, D_ATTN), lambda b, i: (b * nq + i, 2)),
            pl.BlockSpec((seq, D_ATTN), lambda b, i: (b, 3), pipeline_mode=once),
            pl.BlockSpec((seq, D_ATTN), lambda b, i: (b, 4), pipeline_mode=once),
            pl.BlockSpec((seq, LANES), lambda b, i: (b, 0), pipeline_mode=once),
            pl.BlockSpec((Q_TILE, LANES), lambda b, i: (b * nq + i, 0)),
        ],
        out_specs=pl.BlockSpec((Q_TILE, D_ATTN), lambda b, i: (b * nq + i, 0)),
        scratch_shapes=[
            pltpu.VMEM((Q_SUBS, IDX_HEADS * Q_SUB, LANES), BF16),
            pltpu.VMEM((N_HEADS, Q_TILE, LANES), BF16),
            pltpu.VMEM((IDX_HEADS, Q_TILE, LANES), F32),
            pltpu.VMEM((n_kt, Q_TILE, KEY_TILE), F32),
            pltpu.VMEM((n_kt, Q_TILE, KEY_TILE), F32),
            pltpu.VMEM((N_HEADS, Q_TILE, LANES), F32),
            pltpu.VMEM((N_HEADS, Q_TILE, LANES), F32),
            pltpu.VMEM((N_HEADS, Q_TILE, LANES), F32),
            pltpu.VMEM((Q_TILE, LANES), F32),
            pltpu.VMEM((Q_TILE, LANES), I32),
            pltpu.VMEM((Q_TILE, LANES), F32),
            pltpu.VMEM((Q_TILE, LANES), I32),
        ],
        compiler_params=_params(("arbitrary", "arbitrary")),
        name="attn",
    )(p, p, p, p, kk, wi)


MERGE_ROWS = 256


def _rms_rows(x, g):
    ms = jnp.mean(x * x, axis=-1, keepdims=True)
    return (x * lax.rsqrt(ms + EPS)) * g


def _merge_kernel(uc_ref, oa_ref, gc_ref, ga_ref, x_ref, gt_ref, gp_ref,
                  wc_ref, wa_ref, wo_ref, o_ref):
    tm = x_ref.shape[0]
    gt = gt_ref[0]
    gp = gp_ref[...]

    def body(r, carry):
        r0 = pl.multiple_of(r * MERGE_ROWS, MERGE_ROWS)
        rows = pl.ds(r0, MERGE_ROWS)
        yc = _dot(uc_ref[rows, :], wc_ref[...])
        ya = _dot(oa_ref[rows, :], wa_ref[...])
        gc = _sigmoid(gc_ref[rows, :].astype(F32))
        ga = _sigmoid(ga_ref[rows, :].astype(F32))
        merged = (gc * yc + ga * ya).astype(BF16)
        mixed = _dot(merged, wo_ref[...])
        o_ref[rows, :] = x_ref[rows, :] + gt * _rms_rows(mixed, gp)
        return carry

    lax.fori_loop(0, tm // MERGE_ROWS, body, 0)


def _merge(uc, oa, p, x2, gt1, g_post, wc, wa, wo, seq, tm=256):
    m, d = x2.shape
    dc = uc.shape[1]
    per_b = seq // tm
    gate_blk = p.shape[1] // d - 2
    once = pl.Buffered(1)
    return pl.pallas_call(
        _merge_kernel,
        out_shape=jax.ShapeDtypeStruct((m, d), F32),
        grid=(m // tm,),
        in_specs=[
            pl.BlockSpec((tm, dc), lambda i: (i, 0)),
            pl.BlockSpec((tm, dc), lambda i: (i, 0)),
            pl.BlockSpec((tm, d), lambda i: (i, gate_blk)),
            pl.BlockSpec((tm, d), lambda i: (i, gate_blk + 1)),
            pl.BlockSpec((tm, d), lambda i: (i, 0)),
            pl.BlockSpec((1, 1, d), lambda i: (i // per_b, 0, 0)),
            pl.BlockSpec((1, d), lambda i: (0, 0)),
            pl.BlockSpec((dc, d), lambda i: (0, 0), pipeline_mode=once),
            pl.BlockSpec((dc, d), lambda i: (0, 0), pipeline_mode=once),
            pl.BlockSpec((d, d), lambda i: (0, 0), pipeline_mode=once),
        ],
        out_specs=pl.BlockSpec((tm, d), lambda i: (i, 0)),
        compiler_params=_params(("arbitrary",)),
        name="merge",
    )(uc, oa, p, p, x2, gt1, g_post, wc, wa, wo)


FFN_ROWS = 256


def _ffn_kernel(x_ref, g_ref, sc_ref, sh_ref, gt_ref, gp_ref, wg_ref, wu_ref, wd_ref,
                o_ref, h_ref, acc_ref):
    j = pl.program_id(1)
    tm = x_ref.shape[0]
    n_rows = tm // FFN_ROWS

    @pl.when(j == 0)
    def _():
        g = g_ref[...]
        sc = sc_ref[0]
        sh = sh_ref[0]

        def body(r, carry):
            rows = pl.ds(pl.multiple_of(r * ROW_CHUNK, ROW_CHUNK), ROW_CHUNK)
            h_ref[rows, :] = _adaln_rows(x_ref[rows, :], g, sc, sh).astype(BF16)
            acc_ref[rows, :] = jnp.zeros((ROW_CHUNK, acc_ref.shape[1]), F32)
            return carry

        lax.fori_loop(0, tm // ROW_CHUNK, body, 0)

    for r in range(n_rows):
        rows = slice(r * FFN_ROWS, (r + 1) * FFN_ROWS)
        h = h_ref[rows, :]
        a = _dot(h, wg_ref[...])
        b = _dot(h, wu_ref[...])
        act = ((a * _sigmoid(a)) * b).astype(BF16)
        acc_ref[rows, :] = acc_ref[rows, :] + _dot(act, wd_ref[...])

    @pl.when(j == pl.num_programs(1) - 1)
    def _():
        gt = gt_ref[0]
        gp = gp_ref[...]

        def body(r, carry):
            rows = pl.ds(pl.multiple_of(r * ROW_CHUNK, ROW_CHUNK), ROW_CHUNK)
            o_ref[rows, :] = x_ref[rows, :] + gt * _rms_rows(acc_ref[rows, :], gp)
            return carry

        lax.fori_loop(0, tm // ROW_CHUNK, body, 0)


def _ffn(x2, g_pre, sc2, sh2, gt2, g_post, wg, wu, wd, seq, tm=512, tf=512):
    m, d = x2.shape
    dff = wg.shape[1]
    per_b = seq // tm
    return pl.pallas_call(
        _ffn_kernel,
        out_shape=jax.ShapeDtypeStruct((m, d), F32),
        grid=(m // tm, dff // tf),
        in_specs=[
            pl.BlockSpec((tm, d), lambda i, j: (i, 0)),
            pl.BlockSpec((1, d), lambda i, j: (0, 0)),
            pl.BlockSpec((1, 1, d), lambda i, j: (i // per_b, 0, 0)),
            pl.BlockSpec((1, 1, d), lambda i, j: (i // per_b, 0, 0)),
            pl.BlockSpec((1, 1, d), lambda i, j: (i // per_b, 0, 0)),
            pl.BlockSpec((1, d), lambda i, j: (0, 0)),
            pl.BlockSpec((d, tf), lambda i, j: (0, j)),
            pl.BlockSpec((d, tf), lambda i, j: (0, j)),
            pl.BlockSpec((tf, d), lambda i, j: (j, 0)),
        ],
        out_specs=pl.BlockSpec((tm, d), lambda i, j: (i, 0)),
        scratch_shapes=[pltpu.VMEM((tm, d), BF16), pltpu.VMEM((tm, d), F32)],
        compiler_params=_params(("arbitrary", "arbitrary")),
        name="ffn",
    )(x2, g_pre, sc2, sh2, gt2, g_post, wg, wu, wd)


def kernel(x, c, w_ada, b_ada, g_pre_mix, w_in, w_dw, b_dw, g_conv_ln, b_conv_ln, w_conv_out, w_attn_out, w_o, g_post_mix, g_pre_ffn, w_gate, w_up, w_down, g_post_ffn):
    batch, seq, d = x.shape
    depth = w_ada.shape[0]
    topk = min(TOPK_MAX, seq // 4)

    off_glu = 2 * D_CONV
    off_ki = off_glu + 3 * D_ATTN + IDX_HEADS * IDX_DIM
    off_wi = off_ki + IDX_DIM
    off_gate = off_wi + IDX_HEADS

    c8 = jnp.zeros((SUBLANES, d), F32).at[:batch].set(c)
    x2 = x.reshape(batch * seq, d)
    for l in range(depth):
        mod = _ada(c8, w_ada[l], b_ada[l][None, :])[:batch]
        sh1, sc1, gt1, sh2, sc2, gt2 = [t[:, None, :] for t in jnp.split(mod, 6, axis=-1)]

        w = w_in[l]
        w_main = w[:, :off_ki].astype(BF16)
        w_gates = w[:, off_gate:].astype(BF16)
        w_ki = w[:, off_ki:off_wi]
        w_wi = jnp.pad(w[:, off_wi:off_gate], ((0, 0), (0, LANES - IDX_HEADS)))
        w_small = jnp.concatenate([w_ki, w_ki, w_wi], axis=1).astype(BF16)

        p, kk, wi = _proj(x2, g_pre_mix[l][None, :], sc1, sh1, w_main, w_gates, w_small, seq)
        uc = _conv(p, w_dw[l], b_dw[l][None, :], g_conv_ln[l][None, :], b_conv_ln[l][None, :],
                   batch, seq)
        oa = _attn(p, kk, wi, batch, seq, topk)
        x2 = _merge(uc, oa, p, x2, gt1, g_post_mix[l][None, :],
                    w_conv_out[l].astype(BF16), w_attn_out[l].astype(BF16), w_o[l].astype(BF16), seq)
        x2 = _ffn(x2, g_pre_ffn[l][None, :], sc2, sh2, gt2, g_post_ffn[l][None, :],
                  w_gate[l].astype(BF16), w_up[l].astype(BF16), w_down[l].astype(BF16), seq)
    return x2.reshape(batch, seq, d)
```

```python
import functools

import jax
import jax.numpy as jnp
from jax import lax
from jax.experimental import pallas as pl
from jax.experimental.pallas import tpu as pltpu

F32 = jnp.float32
BF16 = jnp.bfloat16
I32 = jnp.int32

CHUNK = 64
CHUNK_SHIFT = 6
D_CONV = 1024
CONV_WIDTH = 31
N_HEADS = 16
HEAD_DIM = 64
D_ATTN = N_HEADS * HEAD_DIM
IDX_HEADS = 16
IDX_DIM = 64
TOPK_MAX = 256
EPS = 1e-6

LANES = 128
SUBLANES = 8
VMEM_LIMIT_BYTES = 56 * 1024 * 1024

LOG2_E = 1.4426950408889634
NEG_BIG = -1e30
INT_MIN = -(2 ** 31)
NEG_INF_KEY = (0xFF800000 ^ 0x7FFFFFFF) - 2 ** 32


def _sigmoid(x):
    return 1.0 / (1.0 + jnp.exp(-x))


def _dot(a, b):
    return jnp.dot(a, b, preferred_element_type=F32)


def _dot_nt(a, b):
    return lax.dot_general(a, b, (((1,), (1,)), ((), ())), preferred_element_type=F32)


def _params(sem):
    return pltpu.CompilerParams(dimension_semantics=sem, vmem_limit_bytes=VMEM_LIMIT_BYTES)


def _ada_kernel(c_ref, w_ref, b_ref, o_ref):
    c = c_ref[...]
    ca = (c * _sigmoid(c)).astype(BF16)
    o_ref[...] = _dot(ca, w_ref[...].astype(BF16)) + b_ref[...]


def _ada(c8, w, b, tn=1024):
    rows, d = c8.shape
    n = w.shape[1]
    return pl.pallas_call(
        _ada_kernel,
        out_shape=jax.ShapeDtypeStruct((rows, n), F32),
        grid=(n // tn,),
        in_specs=[
            pl.BlockSpec((rows, d), lambda j: (0, 0)),
            pl.BlockSpec((d, tn), lambda j: (0, j)),
            pl.BlockSpec((1, tn), lambda j: (0, j)),
        ],
        out_specs=pl.BlockSpec((rows, tn), lambda j: (0, j)),
        compiler_params=_params(("arbitrary",)),
        name="ada",
    )(c8, w, b)


ROW_CHUNK = 128


def _adaln_rows(x, g, sc, sh):
    ms = jnp.mean(x * x, axis=-1, keepdims=True)
    y = (x * lax.rsqrt(ms + EPS)) * g
    return y * (1.0 + sc) + sh


def _proj_kernel(x_ref, g_ref, sc_ref, sh_ref, w_ref, wg_ref, ws_ref, o_ref, kk_ref, wi_ref, h_ref,
                 *, n_main):
    j = pl.program_id(1)
    tm = x_ref.shape[0]

    @pl.when(j == 0)
    def _():
        g = g_ref[...]
        sc = sc_ref[0]
        sh = sh_ref[0]

        def body(r, carry):
            r0 = pl.multiple_of(r * ROW_CHUNK, ROW_CHUNK)
            h = _adaln_rows(x_ref[pl.ds(r0, ROW_CHUNK), :], g, sc, sh).astype(BF16)
            h_ref[pl.ds(r0, ROW_CHUNK), :] = h
            small = _dot(h, ws_ref[...])
            kk_ref[pl.ds(r0, ROW_CHUNK), :] = small[:, :LANES].astype(BF16)
            wi_ref[pl.ds(r0, ROW_CHUNK), :] = small[:, LANES:]
            return carry

        lax.fori_loop(0, tm // ROW_CHUNK, body, 0)

    @pl.when(j < n_main)
    def _():
        o_ref[...] = _dot(h_ref[...], w_ref[...]).astype(BF16)

    @pl.when(j >= n_main)
    def _():
        o_ref[...] = _dot(h_ref[...], wg_ref[...]).astype(BF16)


def _proj(x2, g, sc, sh, w_main, w_gates, w_small, seq, n_main_cols, tm=1024, tn=1024):
    m, d = x2.shape
    n_main = n_main_cols // tn
    n = n_main_cols + w_gates.shape[1]
    per_b = seq // tm
    return pl.pallas_call(
        functools.partial(_proj_kernel, n_main=n_main),
        out_shape=(
            jax.ShapeDtypeStruct((m, n), BF16),
            jax.ShapeDtypeStruct((m, LANES), BF16),
            jax.ShapeDtypeStruct((m, LANES), F32),
        ),
        grid=(m // tm, n // tn),
        in_specs=[
            pl.BlockSpec((tm, d), lambda i, j: (i, 0)),
            pl.BlockSpec((1, d), lambda i, j: (0, 0)),
            pl.BlockSpec((1, 1, d), lambda i, j: (i // per_b, 0, 0)),
            pl.BlockSpec((1, 1, d), lambda i, j: (i // per_b, 0, 0)),
            pl.BlockSpec((d, tn), lambda i, j: (0, jnp.minimum(j, n_main - 1))),
            pl.BlockSpec((d, tn), lambda i, j: (0, jnp.maximum(j - n_main, 0))),
            pl.BlockSpec((d, 2 * LANES), lambda i, j: (0, 0)),
        ],
        out_specs=(
            pl.BlockSpec((tm, tn), lambda i, j: (i, j)),
            pl.BlockSpec((tm, LANES), lambda i, j: (i, 0)),
            pl.BlockSpec((tm, LANES), lambda i, j: (i, 0)),
        ),
        scratch_shapes=[pltpu.VMEM((tm, d), BF16)],
        compiler_params=_params(("arbitrary", "arbitrary")),
        name="proj",
    )(x2, g, sc, sh, w_main, w_gates, w_small)


CONV_HALO = 32
CONV_ROWS = 64


def _conv_kernel(a_ref, gt_ref, w_ref, bdw_ref, gln_ref, bln_ref, o_ref, u_ref, cv_ref, wb_ref,
                 sh_ref):
    s = pl.program_id(1)
    ts = a_ref.shape[0]
    dc = a_ref.shape[1]

    @pl.when(s == 0)
    def _():
        u_ref[0:CONV_HALO, :] = jnp.zeros((CONV_HALO, dc), F32)

    @pl.when(s > 0)
    def _():
        u_ref[0:CONV_HALO, :] = u_ref[ts:ts + CONV_HALO, :]

    def glu_body(r, carry):
        r0 = pl.multiple_of(r * CONV_ROWS, CONV_ROWS)
        a = a_ref[pl.ds(r0, CONV_ROWS), :].astype(F32)
        g = gt_ref[pl.ds(r0, CONV_ROWS), :].astype(F32)
        u_ref[pl.ds(CONV_HALO + r0, CONV_ROWS), :] = a * _sigmoid(g)
        return carry

    lax.fori_loop(0, ts // CONV_ROWS, glu_body, 0)

    for jtap in range(CONV_WIDTH):
        wb_ref[jtap * SUBLANES:(jtap + 1) * SUBLANES, :] = jnp.broadcast_to(
            w_ref[jtap:jtap + 1, :], (SUBLANES, dc))

    base = CONV_HALO - (CONV_WIDTH - 1)

    taps = {}
    for jtap in range(CONV_WIDTH):
        a, b = divmod(base + jtap, SUBLANES)
        taps.setdefault(b, []).append((a, jtap))

    def conv_body(r, carry):
        r0 = pl.multiple_of(r * CONV_ROWS, CONV_ROWS)
        for lc in range(dc // LANES):
            cols = slice(lc * LANES, (lc + 1) * LANES)
            win = u_ref[pl.ds(r0, CONV_ROWS + CONV_HALO), cols]
            for b, group in taps.items():
                span = CONV_ROWS + SUBLANES * max(a for a, _ in group)
                sh_ref[b, 0:span, :] = win[b:b + span, :]
            acc = jnp.zeros((CONV_ROWS, LANES), F32)
            for b, group in taps.items():
                for a, jtap in group:
                    w_tap = wb_ref[jtap * SUBLANES:(jtap + 1) * SUBLANES, cols]
                    w_rows = jnp.concatenate([w_tap] * (CONV_ROWS // SUBLANES), axis=0)
                    acc = acc + sh_ref[b, SUBLANES * a:SUBLANES * a + CONV_ROWS, :] * w_rows
            cv_ref[pl.ds(r0, CONV_ROWS), cols] = acc + bdw_ref[:, cols]
        return carry

    lax.fori_loop(0, ts // CONV_ROWS, conv_body, 0)

    def norm_body(r, carry):
        r0 = pl.multiple_of(r * CONV_ROWS, CONV_ROWS)
        xr = cv_ref[pl.ds(r0, CONV_ROWS), :]
        mu = jnp.mean(xr, axis=-1, keepdims=True)
        xc = xr - mu
        var = jnp.mean(xc * xc, axis=-1, keepdims=True)
        y = (xc * lax.rsqrt(var + EPS)) * gln_ref[...] + bln_ref[...]
        o_ref[pl.ds(r0, CONV_ROWS), :] = (y * _sigmoid(y)).astype(BF16)
        return carry

    lax.fori_loop(0, ts // CONV_ROWS, norm_body, 0)


def _conv(p, w_dw, b_dw, g_ln, b_ln, batch, seq, ts=512):
    m = p.shape[0]
    dc = w_dw.shape[1]
    per_b = seq // ts
    return pl.pallas_call(
        _conv_kernel,
        out_shape=jax.ShapeDtypeStruct((m, dc), BF16),
        grid=(batch, per_b),
        in_specs=[
            pl.BlockSpec((ts, dc), lambda b, s: (b * per_b + s, 0)),
            pl.BlockSpec((ts, dc), lambda b, s: (b * per_b + s, 1)),
            pl.BlockSpec((CONV_WIDTH, dc), lambda b, s: (0, 0)),
            pl.BlockSpec((1, dc), lambda b, s: (0, 0)),
            pl.BlockSpec((1, dc), lambda b, s: (0, 0)),
            pl.BlockSpec((1, dc), lambda b, s: (0, 0)),
        ],
        out_specs=pl.BlockSpec((ts, dc), lambda b, s: (b * per_b + s, 0)),
        scratch_shapes=[
            pltpu.VMEM((ts + CONV_HALO, dc), F32),
            pltpu.VMEM((ts, dc), F32),
            pltpu.VMEM((CONV_WIDTH * SUBLANES, dc), F32),
            pltpu.VMEM((SUBLANES, CONV_ROWS + CONV_HALO, LANES), F32),
        ],
        compiler_params=_params(("arbitrary", "arbitrary")),
        name="conv",
    )(p, p, w_dw, b_dw, g_ln, b_ln)


Q_TILE = 256
Q_SUB = 128
Q_SUBS = Q_TILE // Q_SUB
KEY_TILE = 256
KEY_SUB = 256
SEL_ROWS = 64
LANE_CHUNKS = KEY_TILE // LANES
INDEX_BITS = 13


def _fold_lanes(acc, t, op):
    for ch in range(t.shape[1] // LANES):
        acc = op(acc, t[:, ch * LANES:(ch + 1) * LANES])
    return acc


def _attn_kernel(qi_ref, q_ref, k_ref, v_ref, kk_ref, wi_ref, o_ref,
                 qim_ref, qm_ref, wrep_ref, key_ref, bias_ref, m_ref, acc_ref, thr_ref,
                 cut_ref, cand_ref, idx_ref, *, topk):
    i = pl.program_id(1)
    p0 = i * Q_TILE
    n_kt = (p0 + Q_TILE + KEY_TILE - 1) // KEY_TILE
    idx_scale = (IDX_DIM ** -0.5) * (IDX_HEADS ** -0.5)
    attn_scale = HEAD_DIM ** -0.5

    lane = lax.broadcasted_iota(I32, (Q_TILE, LANES), 1)
    lo = lane < HEAD_DIM
    lo_sub = lax.broadcasted_iota(I32, (Q_SUB, LANES), 1) < HEAD_DIM
    row = lax.broadcasted_iota(I32, (Q_TILE, KEY_TILE), 0)
    col = lax.broadcasted_iota(I32, (Q_TILE, KEY_TILE), 1)
    q_chunk = (p0 + row) >> CHUNK_SHIFT

    wi = wi_ref[...]
    for hp in range(N_HEADS // 2):
        cols = slice(hp * LANES, (hp + 1) * LANES)
        q_p = q_ref[:, cols].astype(F32) * (attn_scale * LOG2_E)
        qm_ref[2 * hp] = jnp.where(lo, q_p, 0.0).astype(BF16)
        qm_ref[2 * hp + 1] = jnp.where(lo, 0.0, q_p).astype(BF16)
        for rs in range(Q_SUBS):
            qi_p = qi_ref[rs * Q_SUB:(rs + 1) * Q_SUB, cols]
            zero_i = jnp.zeros_like(qi_p)
            qim_ref[rs, (2 * hp) * Q_SUB:(2 * hp + 1) * Q_SUB, :] = jnp.where(lo_sub, qi_p, zero_i)
            qim_ref[rs, (2 * hp + 1) * Q_SUB:(2 * hp + 2) * Q_SUB, :] = jnp.where(lo_sub, zero_i, qi_p)
    for h in range(IDX_HEADS):
        wrep_ref[h] = jnp.broadcast_to(wi[:, h:h + 1], (Q_TILE, LANES))

    sub_row = lax.broadcasted_iota(I32, (Q_SUB, KEY_SUB), 0)
    sub_col = lax.broadcasted_iota(I32, (Q_SUB, KEY_SUB), 1)

    def score_body(kt, carry):
        k0 = pl.multiple_of(kt * KEY_TILE, KEY_TILE)
        for ks in range(KEY_TILE // KEY_SUB):
            kk_s = kk_ref[pl.ds(k0 + ks * KEY_SUB, KEY_SUB), :]
            for rs in range(Q_SUBS):
                rows = slice(rs * Q_SUB, (rs + 1) * Q_SUB)
                rel = _dot_nt(qim_ref[rs], kk_s)
                acc = [jnp.zeros((Q_SUB, LANES), F32) for _ in range(KEY_SUB // LANES)]
                for h in range(IDX_HEADS):
                    w_h = wrep_ref[h, rows, :]
                    for ch in range(KEY_SUB // LANES):
                        r = rel[h * Q_SUB:(h + 1) * Q_SUB, ch * LANES:(ch + 1) * LANES]
                        acc[ch] = acc[ch] + jnp.maximum(r, 0.0) * w_h
                score = jnp.concatenate(acc, axis=-1) * idx_scale
                adm = ((k0 + ks * KEY_SUB + sub_col) >> CHUNK_SHIFT) <= (
                    (p0 + rs * Q_SUB + sub_row) >> CHUNK_SHIFT)
                key_ref[kt, rows, ks * KEY_SUB:(ks + 1) * KEY_SUB] = jnp.where(adm, score, -jnp.inf)
        return carry

    lax.fori_loop(0, n_kt, score_body, 0)

    kf = float(topk)

    def key_to_float(key_s):
        bits = key_s ^ ((key_s >> 31) & 0x7FFFFFFF)
        return jnp.where(key_s <= NEG_INF_KEY, -jnp.inf, pltpu.bitcast(bits, F32))

    lane_s = lax.broadcasted_iota(I32, (SEL_ROWS, LANES), 1)

    row_groups = [slice(rg * SEL_ROWS, (rg + 1) * SEL_ROWS) for rg in range(Q_TILE // SEL_ROWS)]

    def count(indicator):
        def body(kt, cs):
            out = []
            for rows, c in zip(row_groups, cs):
                for ch in range(LANE_CHUNKS):
                    kc = key_ref[kt, rows, ch * LANES:(ch + 1) * LANES]
                    c = c + indicator(kc, kt * KEY_TILE + ch * LANES + lane_s, rows)
                out.append(c)
            return tuple(out)

        zero = jnp.zeros((SEL_ROWS, LANES), F32)
        cs = lax.fori_loop(0, n_kt, body, tuple(zero for _ in row_groups))
        return jnp.sum(jnp.concatenate(cs, axis=0), axis=-1, keepdims=True)

    def bisect_body(it, prefix):
        bit = lax.shift_left(jnp.int32(1), jnp.int32(31) - it)
        cand_u = prefix | bit
        cand_ref[...] = key_to_float(cand_u ^ INT_MIN)
        cnt = count(lambda kc, _, rows: jnp.where(kc >= cand_ref[rows, :], 1.0, 0.0))
        return jnp.where(cnt >= kf, cand_u, prefix)

    prefix = lax.fori_loop(0, 32, bisect_body, jnp.zeros((Q_TILE, LANES), I32))
    thr = key_to_float(prefix ^ INT_MIN)
    thr_ref[...] = thr

    n_gt = count(lambda kc, _, rows: jnp.where(kc > thr_ref[rows, :], 1.0, 0.0))
    n_ge = count(lambda kc, _, rows: jnp.where(kc >= thr_ref[rows, :], 1.0, 0.0))
    want = kf - n_gt
    cut_ref[...] = jnp.full((Q_TILE, LANES), 2 ** INDEX_BITS, I32)
    tied = jnp.where(thr[:, :1] == -jnp.inf, 0.0, jnp.where(n_ge > kf, 1.0, 0.0))
    surplus = jnp.max(tied)

    @pl.when(surplus > 0.0)
    def _():
        def cut_body(it, cut):
            bit = lax.shift_left(jnp.int32(1), jnp.int32(INDEX_BITS - 1) - it)
            cand = cut | bit
            idx_ref[...] = cand
            cnt = count(lambda kc, idx, rows: jnp.where(
                kc == thr_ref[rows, :], jnp.where(idx < idx_ref[rows, :], 1.0, 0.0), 0.0))
            return jnp.where(cnt <= want, cand, cut)

        cut_ref[...] = lax.fori_loop(0, INDEX_BITS, cut_body, jnp.zeros((Q_TILE, LANES), I32))

    cut = cut_ref[...]

    def bias_body(kt, carry):
        k0 = pl.multiple_of(kt * KEY_TILE, KEY_TILE)
        kt_keys = key_ref[kt]
        adm = ((k0 + col) >> CHUNK_SHIFT) <= q_chunk
        pieces = []
        for ch in range(LANE_CHUNKS):
            kc = kt_keys[:, ch * LANES:(ch + 1) * LANES]
            idx = k0 + ch * LANES + lane
            tie = jnp.where(kc == thr, jnp.where(idx < cut, 0.0, -jnp.inf), -jnp.inf)
            pieces.append(jnp.where(kc > thr, 0.0, tie))
        b = jnp.concatenate(pieces, axis=-1)
        bias_ref[kt] = jnp.where(adm, b, -jnp.inf)
        return carry

    lax.fori_loop(0, n_kt, bias_body, 0)

    for h in range(N_HEADS):
        m_ref[h] = jnp.full((Q_TILE, LANES), NEG_BIG, F32)
        acc_ref[h] = jnp.zeros((Q_TILE, LANES), F32)

    lo_kv = lax.broadcasted_iota(I32, (KEY_TILE, LANES), 1) < HEAD_DIM

    def attn_body(kt, carry):
        k0 = pl.multiple_of(kt * KEY_TILE, KEY_TILE)
        for hp in range(N_HEADS // 2):
            cols = slice(hp * LANES, (hp + 1) * LANES)
            k_t = k_ref[pl.ds(k0, KEY_TILE), cols]
            v_t = v_ref[pl.ds(k0, KEY_TILE), cols]
            one = jnp.ones_like(v_t)
            v_ext = (jnp.where(lo_kv, v_t, one), jnp.where(lo_kv, one, v_t))
            for par in range(2):
                h = 2 * hp + par
                s = _dot_nt(qm_ref[h], k_t) + bias_ref[kt]
                t_max = jnp.max(_fold_lanes(jnp.full((Q_TILE, LANES), -jnp.inf, F32), s, jnp.maximum),
                                axis=-1, keepdims=True)
                m_old = m_ref[h]
                m_new = jnp.maximum(m_old, t_max)
                alpha = jnp.exp2(m_old - m_new)
                e = jnp.concatenate(
                    [jnp.exp2(s[:, ch * LANES:(ch + 1) * LANES] - m_new) for ch in range(LANE_CHUNKS)],
                    axis=-1)
                acc_ref[h] = alpha * acc_ref[h] + _dot(e.astype(BF16), v_ext[par])
                m_ref[h] = m_new
        return carry

    lax.fori_loop(0, n_kt, attn_body, 0)

    for hp in range(N_HEADS // 2):
        outs = []
        for h in (2 * hp, 2 * hp + 1):
            acc = acc_ref[h]
            outs.append(acc / pltpu.roll(acc, HEAD_DIM, 1))
        o_ref[:, hp * LANES:(hp + 1) * LANES] = jnp.where(lo, outs[0], outs[1]).astype(BF16)


def _attn(p, kk, wi, batch, seq, topk):
    m = p.shape[0]
    nq = seq // Q_TILE
    n_kt = seq // KEY_TILE
    once = pl.Buffered(1)
    return pl.pallas_call(
        functools.partial(_attn_kernel, topk=topk),
        out_shape=jax.ShapeDtypeStruct((m, D_ATTN), BF16),
        grid=(batch, nq),
        in_specs=[
            pl.BlockSpec((Q_TILE, D_ATTN), lambda b, i: (b * nq + i, 5)),
            pl.BlockSpec((Q_TILE, D_ATTN), lambda b, i: (b * nq + i, 2)),
            pl.BlockSpec((seq, D_ATTN), lambda b, i: (b, 3), pipeline_mode=once),
            pl.BlockSpec((seq, D_ATTN), lambda b, i: (b, 4), pipeline_mode=once),
            pl.BlockSpec((seq, LANES), lambda b, i: (b, 0), pipeline_mode=once),
            pl.BlockSpec((Q_TILE, LANES), lambda b, i: (b * nq + i, 0)),
        ],
        out_specs=pl.BlockSpec((Q_TILE, D_ATTN), lambda b, i: (b * nq + i, 0)),
        scratch_shapes=[
            pltpu.VMEM((Q_SUBS, IDX_HEADS * Q_SUB, LANES), BF16),
            pltpu.VMEM((N_HEADS, Q_TILE, LANES), BF16),
            pltpu.VMEM((IDX_HEADS, Q_TILE, LANES), F32),
            pltpu.VMEM((n_kt, Q_TILE, KEY_TILE), F32),
            pltpu.VMEM((n_kt, Q_TILE, KEY_TILE), F32),
            pltpu.VMEM((N_HEADS, Q_TILE, LANES), F32),
            pltpu.VMEM((N_HEADS, Q_TILE, LANES), F32),
            pltpu.VMEM((Q_TILE, LANES), F32),
            pltpu.VMEM((Q_TILE, LANES), I32),
            pltpu.VMEM((Q_TILE, LANES), F32),
            pltpu.VMEM((Q_TILE, LANES), I32),
        ],
        compiler_params=_params(("arbitrary", "arbitrary")),
        name="attn",
    )(p, p, p, p, kk, wi)


MERGE_ROWS = 256


def _rms_rows(x, g):
    ms = jnp.mean(x * x, axis=-1, keepdims=True)
    return (x * lax.rsqrt(ms + EPS)) * g


def _merge_kernel(uc_ref, oa_ref, gc_ref, ga_ref, x_ref, gt_ref, gp_ref,
                  wc_ref, wa_ref, wo_ref, o_ref):
    tm = x_ref.shape[0]
    gt = gt_ref[0]
    gp = gp_ref[...]

    def body(r, carry):
        r0 = pl.multiple_of(r * MERGE_ROWS, MERGE_ROWS)
        rows = pl.ds(r0, MERGE_ROWS)
        yc = _dot(uc_ref[rows, :], wc_ref[...])
        ya = _dot(oa_ref[rows, :], wa_ref[...])
        gc = _sigmoid(gc_ref[rows, :].astype(F32))
        ga = _sigmoid(ga_ref[rows, :].astype(F32))
        merged = (gc * yc + ga * ya).astype(BF16)
        mixed = _dot(merged, wo_ref[...])
        o_ref[rows, :] = x_ref[rows, :] + gt * _rms_rows(mixed, gp)
        return carry

    lax.fori_loop(0, tm // MERGE_ROWS, body, 0)


def _merge(uc, oa, p, x2, gt1, g_post, wc, wa, wo, seq, tm=256):
    m, d = x2.shape
    dc = uc.shape[1]
    per_b = seq // tm
    gate_blk = p.shape[1] // d - 2
    once = pl.Buffered(1)
    return pl.pallas_call(
        _merge_kernel,
        out_shape=jax.ShapeDtypeStruct((m, d), F32),
        grid=(m // tm,),
        in_specs=[
            pl.BlockSpec((tm, dc), lambda i: (i, 0)),
            pl.BlockSpec((tm, dc), lambda i: (i, 0)),
            pl.BlockSpec((tm, d), lambda i: (i, gate_blk)),
            pl.BlockSpec((tm, d), lambda i: (i, gate_blk + 1)),
            pl.BlockSpec((tm, d), lambda i: (i, 0)),
            pl.BlockSpec((1, 1, d), lambda i: (i // per_b, 0, 0)),
            pl.BlockSpec((1, d), lambda i: (0, 0)),
            pl.BlockSpec((dc, d), lambda i: (0, 0), pipeline_mode=once),
            pl.BlockSpec((dc, d), lambda i: (0, 0), pipeline_mode=once),
            pl.BlockSpec((d, d), lambda i: (0, 0), pipeline_mode=once),
        ],
        out_specs=pl.BlockSpec((tm, d), lambda i: (i, 0)),
        compiler_params=_params(("arbitrary",)),
        name="merge",
    )(uc, oa, p, p, x2, gt1, g_post, wc, wa, wo)


FFN_ROWS = 256


def _ffn_kernel(x_ref, g_ref, sc_ref, sh_ref, gt_ref, gp_ref, wg_ref, wu_ref, wd_ref,
                o_ref, h_ref, acc_ref):
    j = pl.program_id(1)
    tm = x_ref.shape[0]
    n_rows = tm // FFN_ROWS

    @pl.when(j == 0)
    def _():
        g = g_ref[...]
        sc = sc_ref[0]
        sh = sh_ref[0]

        def body(r, carry):
            rows = pl.ds(pl.multiple_of(r * ROW_CHUNK, ROW_CHUNK), ROW_CHUNK)
            h_ref[rows, :] = _adaln_rows(x_ref[rows, :], g, sc, sh).astype(BF16)
            acc_ref[rows, :] = jnp.zeros((ROW_CHUNK, acc_ref.shape[1]), F32)
            return carry

        lax.fori_loop(0, tm // ROW_CHUNK, body, 0)

    for r in range(n_rows):
        rows = slice(r * FFN_ROWS, (r + 1) * FFN_ROWS)
        h = h_ref[rows, :]
        a = _dot(h, wg_ref[...])
        b = _dot(h, wu_ref[...])
        act = ((a * _sigmoid(a)) * b).astype(BF16)
        acc_ref[rows, :] = acc_ref[rows, :] + _dot(act, wd_ref[...])

    @pl.when(j == pl.num_programs(1) - 1)
    def _():
        gt = gt_ref[0]
        gp = gp_ref[...]

        def body(r, carry):
            rows = pl.ds(pl.multiple_of(r * ROW_CHUNK, ROW_CHUNK), ROW_CHUNK)
            o_ref[rows, :] = x_ref[rows, :] + gt * _rms_rows(acc_ref[rows, :], gp)
            return carry

        lax.fori_loop(0, tm // ROW_CHUNK, body, 0)


def _ffn(x2, g_pre, sc2, sh2, gt2, g_post, wg, wu, wd, seq, tm=512, tf=512):
    m, d = x2.shape
    dff = wg.shape[1]
    per_b = seq // tm
    return pl.pallas_call(
        _ffn_kernel,
        out_shape=jax.ShapeDtypeStruct((m, d), F32),
        grid=(m // tm, dff // tf),
        in_specs=[
            pl.BlockSpec((tm, d), lambda i, j: (i, 0)),
            pl.BlockSpec((1, d), lambda i, j: (0, 0)),
            pl.BlockSpec((1, 1, d), lambda i, j: (i // per_b, 0, 0)),
            pl.BlockSpec((1, 1, d), lambda i, j: (i // per_b, 0, 0)),
            pl.BlockSpec((1, 1, d), lambda i, j: (i // per_b, 0, 0)),
            pl.BlockSpec((1, d), lambda i, j: (0, 0)),
            pl.BlockSpec((d, tf), lambda i, j: (0, j)),
            pl.BlockSpec((d, tf), lambda i, j: (0, j)),
            pl.BlockSpec((tf, d), lambda i, j: (j, 0)),
        ],
        out_specs=pl.BlockSpec((tm, d), lambda i, j: (i, 0)),
        scratch_shapes=[pltpu.VMEM((tm, d), BF16), pltpu.VMEM((tm, d), F32)],
        compiler_params=_params(("arbitrary", "arbitrary")),
        name="ffn",
    )(x2, g_pre, sc2, sh2, gt2, g_post, wg, wu, wd)


def kernel(x, c, w_ada, b_ada, g_pre_mix, w_in, w_dw, b_dw, g_conv_ln, b_conv_ln, w_conv_out, w_attn_out, w_o, g_post_mix, g_pre_ffn, w_gate, w_up, w_down, g_post_ffn):
    batch, seq, d = x.shape
    depth = w_ada.shape[0]
    topk = min(TOPK_MAX, seq // 4)

    off_glu = 2 * D_CONV
    off_ki = off_glu + 3 * D_ATTN + IDX_HEADS * IDX_DIM
    off_wi = off_ki + IDX_DIM
    off_gate = off_wi + IDX_HEADS

    c8 = jnp.zeros((SUBLANES, d), F32).at[:batch].set(c)
    x2 = x.reshape(batch * seq, d)
    for l in range(depth):
        mod = _ada(c8, w_ada[l], b_ada[l][None, :])[:batch]
        sh1, sc1, gt1, sh2, sc2, gt2 = [t[:, None, :] for t in jnp.split(mod, 6, axis=-1)]

        w = w_in[l]
        w_bf = w.astype(BF16)
        w_gates = w_bf[:, off_gate:]
        w_ki = w_bf[:, off_ki:off_wi]
        w_wi = jnp.pad(w_bf[:, off_wi:off_gate], ((0, 0), (0, LANES - IDX_HEADS)))
        w_small = jnp.concatenate([w_ki, w_ki, w_wi], axis=1)

        p, kk, wi = _proj(x2, g_pre_mix[l][None, :], sc1, sh1, w_bf, w_gates, w_small, seq,
                          n_main_cols=off_ki)
        uc = _conv(p, w_dw[l], b_dw[l][None, :], g_conv_ln[l][None, :], b_conv_ln[l][None, :],
                   batch, seq)
        oa = _attn(p, kk, wi, batch, seq, topk)
        x2 = _merge(uc, oa, p, x2, gt1, g_post_mix[l][None, :],
                    w_conv_out[l].astype(BF16), w_attn_out[l].astype(BF16), w_o[l].astype(BF16), seq)
        x2 = _ffn(x2, g_pre_ffn[l][None, :], sc2, sh2, gt2, g_post_ffn[l][None, :],
                  w_gate[l].astype(BF16), w_up[l].astype(BF16), w_down[l].astype(BF16), seq)
    return x2.reshape(batch, seq, d)
```

```python
import functools

import jax
import jax.numpy as jnp
from jax import lax
from jax.experimental import pallas as pl
from jax.experimental.pallas import tpu as pltpu

F32 = jnp.float32
BF16 = jnp.bfloat16
I32 = jnp.int32

CHUNK = 64
CHUNK_SHIFT = 6
D_CONV = 1024
CONV_WIDTH = 31
N_HEADS = 16
HEAD_DIM = 64
D_ATTN = N_HEADS * HEAD_DIM
IDX_HEADS = 16
IDX_DIM = 64
TOPK_MAX = 256
EPS = 1e-6

LANES = 128
SUBLANES = 8
VMEM_LIMIT_BYTES = 56 * 1024 * 1024

LOG2_E = 1.4426950408889634
NEG_BIG = -1e30
INT_MIN = -(2 ** 31)
NEG_INF_KEY = (0xFF800000 ^ 0x7FFFFFFF) - 2 ** 32


def _sigmoid(x):
    return 1.0 / (1.0 + jnp.exp(-x))


def _dot(a, b):
    return jnp.dot(a, b, preferred_element_type=F32)


def _dot_nt(a, b):
    return lax.dot_general(a, b, (((1,), (1,)), ((), ())), preferred_element_type=F32)


def _params(sem):
    return pltpu.CompilerParams(dimension_semantics=sem, vmem_limit_bytes=VMEM_LIMIT_BYTES)


def _ada_kernel(c_ref, w_ref, b_ref, o_ref):
    c = c_ref[...]
    ca = (c * _sigmoid(c)).astype(BF16)
    o_ref[...] = _dot(ca, w_ref[...].astype(BF16)) + b_ref[...]


def _ada(c8, w, b, tn=1024):
    rows, d = c8.shape
    n = w.shape[1]
    return pl.pallas_call(
        _ada_kernel,
        out_shape=jax.ShapeDtypeStruct((rows, n), F32),
        grid=(n // tn,),
        in_specs=[
            pl.BlockSpec((rows, d), lambda j: (0, 0)),
            pl.BlockSpec((d, tn), lambda j: (0, j)),
            pl.BlockSpec((1, tn), lambda j: (0, j)),
        ],
        out_specs=pl.BlockSpec((rows, tn), lambda j: (0, j)),
        compiler_params=_params(("arbitrary",)),
        name="ada",
    )(c8, w, b)


ROW_CHUNK = 128


def _adaln_rows(x, g, sc, sh):
    ms = jnp.mean(x * x, axis=-1, keepdims=True)
    y = (x * lax.rsqrt(ms + EPS)) * g
    return y * (1.0 + sc) + sh


def _proj_kernel(x_ref, g_ref, sc_ref, sh_ref, w_ref, wg_ref, ws_ref, o_ref, kk_ref, wi_ref, h_ref,
                 *, n_main):
    j = pl.program_id(1)
    tm = x_ref.shape[0]

    @pl.when(j == 0)
    def _():
        g = g_ref[...]
        sc = sc_ref[0]
        sh = sh_ref[0]

        def body(r, carry):
            r0 = pl.multiple_of(r * ROW_CHUNK, ROW_CHUNK)
            h = _adaln_rows(x_ref[pl.ds(r0, ROW_CHUNK), :], g, sc, sh).astype(BF16)
            h_ref[pl.ds(r0, ROW_CHUNK), :] = h
            small = _dot(h, ws_ref[...])
            kk_ref[pl.ds(r0, ROW_CHUNK), :] = small[:, :LANES].astype(BF16)
            wi_ref[pl.ds(r0, ROW_CHUNK), :] = small[:, LANES:]
            return carry

        lax.fori_loop(0, tm // ROW_CHUNK, body, 0)

    @pl.when(j < n_main)
    def _():
        o_ref[...] = _dot(h_ref[...], w_ref[...]).astype(BF16)

    @pl.when(j >= n_main)
    def _():
        o_ref[...] = _dot(h_ref[...], wg_ref[...]).astype(BF16)


def _proj(x2, g, sc, sh, w_main, w_gates, w_small, seq, n_main_cols, tm=1024, tn=1024):
    m, d = x2.shape
    n_main = n_main_cols // tn
    n = n_main_cols + w_gates.shape[1]
    per_b = seq // tm
    return pl.pallas_call(
        functools.partial(_proj_kernel, n_main=n_main),
        out_shape=(
            jax.ShapeDtypeStruct((m, n), BF16),
            jax.ShapeDtypeStruct((m, LANES), BF16),
            jax.ShapeDtypeStruct((m, LANES), F32),
        ),
        grid=(m // tm, n // tn),
        in_specs=[
            pl.BlockSpec((tm, d), lambda i, j: (i, 0)),
            pl.BlockSpec((1, d), lambda i, j: (0, 0)),
            pl.BlockSpec((1, 1, d), lambda i, j: (i // per_b, 0, 0)),
            pl.BlockSpec((1, 1, d), lambda i, j: (i // per_b, 0, 0)),
            pl.BlockSpec((d, tn), lambda i, j: (0, jnp.minimum(j, n_main - 1))),
            pl.BlockSpec((d, tn), lambda i, j: (0, jnp.maximum(j - n_main, 0))),
            pl.BlockSpec((d, 2 * LANES), lambda i, j: (0, 0)),
        ],
        out_specs=(
            pl.BlockSpec((tm, tn), lambda i, j: (i, j)),
            pl.BlockSpec((tm, LANES), lambda i, j: (i, 0)),
            pl.BlockSpec((tm, LANES), lambda i, j: (i, 0)),
        ),
        scratch_shapes=[pltpu.VMEM((tm, d), BF16)],
        compiler_params=_params(("arbitrary", "arbitrary")),
        name="proj",
    )(x2, g, sc, sh, w_main, w_gates, w_small)


CONV_HALO = 32
CONV_ROWS = 64
NORM_ROWS = 256


def _conv_kernel(a_ref, gt_ref, w_ref, bdw_ref, gln_ref, bln_ref, o_ref, u_ref, cv_ref, wb_ref,
                 sh_ref):
    s = pl.program_id(1)
    ts = a_ref.shape[0]
    dc = a_ref.shape[1]

    @pl.when(s == 0)
    def _():
        u_ref[0:CONV_HALO, :] = jnp.zeros((CONV_HALO, dc), F32)

    @pl.when(s > 0)
    def _():
        u_ref[0:CONV_HALO, :] = u_ref[ts:ts + CONV_HALO, :]

    def glu_body(r, carry):
        r0 = pl.multiple_of(r * CONV_ROWS, CONV_ROWS)
        a = a_ref[pl.ds(r0, CONV_ROWS), :].astype(F32)
        g = gt_ref[pl.ds(r0, CONV_ROWS), :].astype(F32)
        u_ref[pl.ds(CONV_HALO + r0, CONV_ROWS), :] = a * _sigmoid(g)
        return carry

    lax.fori_loop(0, ts // CONV_ROWS, glu_body, 0)

    for jtap in range(CONV_WIDTH):
        wb_ref[jtap * SUBLANES:(jtap + 1) * SUBLANES, :] = jnp.broadcast_to(
            w_ref[jtap:jtap + 1, :], (SUBLANES, dc))

    base = CONV_HALO - (CONV_WIDTH - 1)

    taps = {}
    for jtap in range(CONV_WIDTH):
        a, b = divmod(base + jtap, SUBLANES)
        taps.setdefault(b, []).append((a, jtap))

    def conv_body(r, carry):
        r0 = pl.multiple_of(r * CONV_ROWS, CONV_ROWS)
        for lc in range(dc // LANES):
            cols = slice(lc * LANES, (lc + 1) * LANES)
            win = u_ref[pl.ds(r0, CONV_ROWS + CONV_HALO), cols]
            for b, group in taps.items():
                span = CONV_ROWS + SUBLANES * max(a for a, _ in group)
                sh_ref[lc, b, 0:span, :] = win[b:b + span, :]
            acc = jnp.zeros((CONV_ROWS, LANES), F32)
            for b, group in taps.items():
                for a, jtap in group:
                    w_tap = wb_ref[jtap * SUBLANES:(jtap + 1) * SUBLANES, cols]
                    w_rows = jnp.concatenate([w_tap] * (CONV_ROWS // SUBLANES), axis=0)
                    acc = acc + sh_ref[lc, b, SUBLANES * a:SUBLANES * a + CONV_ROWS, :] * w_rows
            cv_ref[pl.ds(r0, CONV_ROWS), cols] = acc + bdw_ref[:, cols]
        return carry

    lax.fori_loop(0, ts // CONV_ROWS, conv_body, 0)

    def norm_body(r, carry):
        r0 = pl.multiple_of(r * NORM_ROWS, NORM_ROWS)
        xr = cv_ref[pl.ds(r0, NORM_ROWS), :]
        mu = jnp.mean(xr, axis=-1, keepdims=True)
        xc = xr - mu
        var = jnp.mean(xc * xc, axis=-1, keepdims=True)
        y = (xc * lax.rsqrt(var + EPS)) * gln_ref[...] + bln_ref[...]
        o_ref[pl.ds(r0, NORM_ROWS), :] = (y * _sigmoid(y)).astype(BF16)
        return carry

    lax.fori_loop(0, ts // NORM_ROWS, norm_body, 0)


def _conv(p, w_dw, b_dw, g_ln, b_ln, batch, seq, ts=512):
    m = p.shape[0]
    dc = w_dw.shape[1]
    per_b = seq // ts
    return pl.pallas_call(
        _conv_kernel,
        out_shape=jax.ShapeDtypeStruct((m, dc), BF16),
        grid=(batch, per_b),
        in_specs=[
            pl.BlockSpec((ts, dc), lambda b, s: (b * per_b + s, 0)),
            pl.BlockSpec((ts, dc), lambda b, s: (b * per_b + s, 1)),
            pl.BlockSpec((CONV_WIDTH, dc), lambda b, s: (0, 0)),
            pl.BlockSpec((1, dc), lambda b, s: (0, 0)),
            pl.BlockSpec((1, dc), lambda b, s: (0, 0)),
            pl.BlockSpec((1, dc), lambda b, s: (0, 0)),
        ],
        out_specs=pl.BlockSpec((ts, dc), lambda b, s: (b * per_b + s, 0)),
        scratch_shapes=[
            pltpu.VMEM((ts + CONV_HALO, dc), F32),
            pltpu.VMEM((ts, dc), F32),
            pltpu.VMEM((CONV_WIDTH * SUBLANES, dc), F32),
            pltpu.VMEM((dc // LANES, SUBLANES, CONV_ROWS + CONV_HALO, LANES), F32),
        ],
        compiler_params=_params(("arbitrary", "arbitrary")),
        name="conv",
    )(p, p, w_dw, b_dw, g_ln, b_ln)


Q_TILE = 256
Q_SUB = 128
Q_SUBS = Q_TILE // Q_SUB
KEY_TILE = 256
KEY_SUB = 256
SEL_ROWS = 64
LANE_CHUNKS = KEY_TILE // LANES
INDEX_BITS = 13


def _fold_lanes(acc, t, op):
    for ch in range(t.shape[1] // LANES):
        acc = op(acc, t[:, ch * LANES:(ch + 1) * LANES])
    return acc


def _attn_kernel(qi_ref, q_ref, k_ref, v_ref, kk_ref, wi_ref, o_ref,
                 qim_ref, qm_ref, wrep_ref, key_ref, bias_ref, m_ref, acc_ref, thr_ref,
                 cut_ref, cand_ref, idx_ref, *, topk):
    i = pl.program_id(1)
    p0 = i * Q_TILE
    n_kt = (p0 + Q_TILE + KEY_TILE - 1) // KEY_TILE
    idx_scale = (IDX_DIM ** -0.5) * (IDX_HEADS ** -0.5)
    attn_scale = HEAD_DIM ** -0.5

    lane = lax.broadcasted_iota(I32, (Q_TILE, LANES), 1)
    lo = lane < HEAD_DIM
    lo_sub = lax.broadcasted_iota(I32, (Q_SUB, LANES), 1) < HEAD_DIM
    row = lax.broadcasted_iota(I32, (Q_TILE, KEY_TILE), 0)
    col = lax.broadcasted_iota(I32, (Q_TILE, KEY_TILE), 1)
    q_chunk = (p0 + row) >> CHUNK_SHIFT

    wi = wi_ref[...]
    for hp in range(N_HEADS // 2):
        cols = slice(hp * LANES, (hp + 1) * LANES)
        q_p = q_ref[:, cols].astype(F32) * (attn_scale * LOG2_E)
        qm_ref[2 * hp] = jnp.where(lo, q_p, 0.0).astype(BF16)
        qm_ref[2 * hp + 1] = jnp.where(lo, 0.0, q_p).astype(BF16)
        for rs in range(Q_SUBS):
            qi_p = qi_ref[rs * Q_SUB:(rs + 1) * Q_SUB, cols]
            zero_i = jnp.zeros_like(qi_p)
            qim_ref[rs, (2 * hp) * Q_SUB:(2 * hp + 1) * Q_SUB, :] = jnp.where(lo_sub, qi_p, zero_i)
            qim_ref[rs, (2 * hp + 1) * Q_SUB:(2 * hp + 2) * Q_SUB, :] = jnp.where(lo_sub, zero_i, qi_p)
    for h in range(IDX_HEADS):
        wrep_ref[h] = jnp.broadcast_to(wi[:, h:h + 1], (Q_TILE, LANES))

    sub_row = lax.broadcasted_iota(I32, (Q_SUB, KEY_SUB), 0)
    sub_col = lax.broadcasted_iota(I32, (Q_SUB, KEY_SUB), 1)

    def score_body(kt, carry):
        k0 = pl.multiple_of(kt * KEY_TILE, KEY_TILE)
        for ks in range(KEY_TILE // KEY_SUB):
            kk_s = kk_ref[pl.ds(k0 + ks * KEY_SUB, KEY_SUB), :]
            for rs in range(Q_SUBS):
                rows = slice(rs * Q_SUB, (rs + 1) * Q_SUB)
                rel = _dot_nt(qim_ref[rs], kk_s)
                acc = [jnp.zeros((Q_SUB, LANES), F32) for _ in range(KEY_SUB // LANES)]
                for h in range(IDX_HEADS):
                    w_h = wrep_ref[h, rows, :]
                    for ch in range(KEY_SUB // LANES):
                        r = rel[h * Q_SUB:(h + 1) * Q_SUB, ch * LANES:(ch + 1) * LANES]
                        acc[ch] = acc[ch] + jnp.maximum(r, 0.0) * w_h
                score = jnp.concatenate(acc, axis=-1) * idx_scale
                adm = ((k0 + ks * KEY_SUB + sub_col) >> CHUNK_SHIFT) <= (
                    (p0 + rs * Q_SUB + sub_row) >> CHUNK_SHIFT)
                key_ref[kt, rows, ks * KEY_SUB:(ks + 1) * KEY_SUB] = jnp.where(adm, score, -jnp.inf)
        return carry

    lax.fori_loop(0, n_kt, score_body, 0)

    kf = float(topk)

    def key_to_float(key_s):
        bits = key_s ^ ((key_s >> 31) & 0x7FFFFFFF)
        return jnp.where(key_s <= NEG_INF_KEY, -jnp.inf, pltpu.bitcast(bits, F32))

    lane_s = lax.broadcasted_iota(I32, (SEL_ROWS, LANES), 1)

    row_groups = [slice(rg * SEL_ROWS, (rg + 1) * SEL_ROWS) for rg in range(Q_TILE // SEL_ROWS)]

    def count(indicator):
        def body(kt, cs):
            out = []
            for rows, c in zip(row_groups, cs):
                for ch in range(LANE_CHUNKS):
                    kc = key_ref[kt, rows, ch * LANES:(ch + 1) * LANES]
                    c = c + indicator(kc, kt * KEY_TILE + ch * LANES + lane_s, rows)
                out.append(c)
            return tuple(out)

        zero = jnp.zeros((SEL_ROWS, LANES), F32)
        cs = lax.fori_loop(0, n_kt, body, tuple(zero for _ in row_groups))
        return jnp.sum(jnp.concatenate(cs, axis=0), axis=-1, keepdims=True)

    def bisect_body(it, prefix):
        bit = lax.shift_left(jnp.int32(1), jnp.int32(31) - it)
        cand_u = prefix | bit
        cand_ref[...] = key_to_float(cand_u ^ INT_MIN)
        cnt = count(lambda kc, _, rows: jnp.where(kc >= cand_ref[rows, :], 1.0, 0.0))
        return jnp.where(cnt >= kf, cand_u, prefix)

    prefix = lax.fori_loop(0, 32, bisect_body, jnp.zeros((Q_TILE, LANES), I32))
    thr = key_to_float(prefix ^ INT_MIN)
    thr_ref[...] = thr

    n_gt = count(lambda kc, _, rows: jnp.where(kc > thr_ref[rows, :], 1.0, 0.0))
    n_ge = count(lambda kc, _, rows: jnp.where(kc >= thr_ref[rows, :], 1.0, 0.0))
    want = kf - n_gt
    cut_ref[...] = jnp.full((Q_TILE, LANES), 2 ** INDEX_BITS, I32)
    tied = jnp.where(thr[:, :1] == -jnp.inf, 0.0, jnp.where(n_ge > kf, 1.0, 0.0))
    surplus = jnp.max(tied)

    @pl.when(surplus > 0.0)
    def _():
        def cut_body(it, cut):
            bit = lax.shift_left(jnp.int32(1), jnp.int32(INDEX_BITS - 1) - it)
            cand = cut | bit
            idx_ref[...] = cand
            cnt = count(lambda kc, idx, rows: jnp.where(
                kc == thr_ref[rows, :], jnp.where(idx < idx_ref[rows, :], 1.0, 0.0), 0.0))
            return jnp.where(cnt <= want, cand, cut)

        cut_ref[...] = lax.fori_loop(0, INDEX_BITS, cut_body, jnp.zeros((Q_TILE, LANES), I32))

    cut = cut_ref[...]

    def bias_body(kt, carry):
        k0 = pl.multiple_of(kt * KEY_TILE, KEY_TILE)
        kt_keys = key_ref[kt]
        adm = ((k0 + col) >> CHUNK_SHIFT) <= q_chunk
        pieces = []
        for ch in range(LANE_CHUNKS):
            kc = kt_keys[:, ch * LANES:(ch + 1) * LANES]
            idx = k0 + ch * LANES + lane
            tie = jnp.where(kc == thr, jnp.where(idx < cut, 0.0, -jnp.inf), -jnp.inf)
            pieces.append(jnp.where(kc > thr, 0.0, tie))
        b = jnp.concatenate(pieces, axis=-1)
        bias_ref[kt] = jnp.where(adm, b, -jnp.inf)
        return carry

    lax.fori_loop(0, n_kt, bias_body, 0)

    for h in range(N_HEADS):
        m_ref[h] = jnp.full((Q_TILE, LANES), NEG_BIG, F32)
        acc_ref[h] = jnp.zeros((Q_TILE, LANES), F32)

    lo_kv = lax.broadcasted_iota(I32, (KEY_TILE, LANES), 1) < HEAD_DIM

    def attn_body(kt, carry):
        k0 = pl.multiple_of(kt * KEY_TILE, KEY_TILE)
        for hp in range(N_HEADS // 2):
            cols = slice(hp * LANES, (hp + 1) * LANES)
            k_t = k_ref[pl.ds(k0, KEY_TILE), cols]
            v_t = v_ref[pl.ds(k0, KEY_TILE), cols]
            one = jnp.ones_like(v_t)
            v_ext = (jnp.where(lo_kv, v_t, one), jnp.where(lo_kv, one, v_t))
            for par in range(2):
                h = 2 * hp + par
                s = _dot_nt(qm_ref[h], k_t) + bias_ref[kt]
                t_max = jnp.max(_fold_lanes(jnp.full((Q_TILE, LANES), -jnp.inf, F32), s, jnp.maximum),
                                axis=-1, keepdims=True)
                m_old = m_ref[h]
                m_new = jnp.maximum(m_old, t_max)
                alpha = jnp.exp2(m_old - m_new)
                e = jnp.concatenate(
                    [jnp.exp2(s[:, ch * LANES:(ch + 1) * LANES] - m_new) for ch in range(LANE_CHUNKS)],
                    axis=-1)
                acc_ref[h] = alpha * acc_ref[h] + _dot(e.astype(BF16), v_ext[par])
                m_ref[h] = m_new
        return carry

    lax.fori_loop(0, n_kt, attn_body, 0)

    for hp in range(N_HEADS // 2):
        outs = []
        for h in (2 * hp, 2 * hp + 1):
            acc = acc_ref[h]
            outs.append(acc / pltpu.roll(acc, HEAD_DIM, 1))
        o_ref[:, hp * LANES:(hp + 1) * LANES] = jnp.where(lo, outs[0], outs[1]).astype(BF16)


def _attn(p, kk, wi, batch, seq, topk):
    m = p.shape[0]
    nq = seq // Q_TILE
    n_kt = seq // KEY_TILE
    once = pl.Buffered(1)
    return pl.pallas_call(
        functools.partial(_attn_kernel, topk=topk),
        out_shape=jax.ShapeDtypeStruct((m, D_ATTN), BF16),
        grid=(batch, nq),
        in_specs=[
            pl.BlockSpec((Q_TILE, D_ATTN), lambda b, i: (b * nq + i, 5)),
            pl.BlockSpec((Q_TILE, D_ATTN), lambda b, i: (b * nq + i, 2)),
            pl.BlockSpec((seq, D_ATTN), lambda b, i: (b, 3), pipeline_mode=once),
            pl.BlockSpec((seq, D_ATTN), lambda b, i: (b, 4), pipeline_mode=once),
            pl.BlockSpec((seq, LANES), lambda b, i: (b, 0), pipeline_mode=once),
            pl.BlockSpec((Q_TILE, LANES), lambda b, i: (b * nq + i, 0)),
        ],
        out_specs=pl.BlockSpec((Q_TILE, D_ATTN), lambda b, i: (b * nq + i, 0)),
        scratch_shapes=[
            pltpu.VMEM((Q_SUBS, IDX_HEADS * Q_SUB, LANES), BF16),
            pltpu.VMEM((N_HEADS, Q_TILE, LANES), BF16),
            pltpu.VMEM((IDX_HEADS, Q_TILE, LANES), F32),
            pltpu.VMEM((n_kt, Q_TILE, KEY_TILE), F32),
            pltpu.VMEM((n_kt, Q_TILE, KEY_TILE), F32),
            pltpu.VMEM((N_HEADS, Q_TILE, LANES), F32),
            pltpu.VMEM((N_HEADS, Q_TILE, LANES), F32),
            pltpu.VMEM((Q_TILE, LANES), F32),
            pltpu.VMEM((Q_TILE, LANES), I32),
            pltpu.VMEM((Q_TILE, LANES), F32),
            pltpu.VMEM((Q_TILE, LANES), I32),
        ],
        compiler_params=_params(("arbitrary", "arbitrary")),
        name="attn",
    )(p, p, p, p, kk, wi)


MERGE_ROWS = 256


def _rms_rows(x, g):
    ms = jnp.mean(x * x, axis=-1, keepdims=True)
    return (x * lax.rsqrt(ms + EPS)) * g


def _merge_kernel(uc_ref, oa_ref, gc_ref, ga_ref, x_ref, gt_ref, gp_ref,
                  wc_ref, wa_ref, wo_ref, o_ref):
    tm = x_ref.shape[0]
    gt = gt_ref[0]
    gp = gp_ref[...]

    def body(r, carry):
        r0 = pl.multiple_of(r * MERGE_ROWS, MERGE_ROWS)
        rows = pl.ds(r0, MERGE_ROWS)
        yc = _dot(uc_ref[rows, :], wc_ref[...])
        ya = _dot(oa_ref[rows, :], wa_ref[...])
        gc = _sigmoid(gc_ref[rows, :].astype(F32))
        ga = _sigmoid(ga_ref[rows, :].astype(F32))
        merged = (gc * yc + ga * ya).astype(BF16)
        mixed = _dot(merged, wo_ref[...])
        o_ref[rows, :] = x_ref[rows, :] + gt * _rms_rows(mixed, gp)
        return carry

    lax.fori_loop(0, tm // MERGE_ROWS, body, 0)


def _merge(uc, oa, p, x2, gt1, g_post, wc, wa, wo, seq, tm=256):
    m, d = x2.shape
    dc = uc.shape[1]
    per_b = seq // tm
    gate_blk = p.shape[1] // d - 2
    once = pl.Buffered(1)
    return pl.pallas_call(
        _merge_kernel,
        out_shape=jax.ShapeDtypeStruct((m, d), F32),
        grid=(m // tm,),
        in_specs=[
            pl.BlockSpec((tm, dc), lambda i: (i, 0)),
            pl.BlockSpec((tm, dc), lambda i: (i, 0)),
            pl.BlockSpec((tm, d), lambda i: (i, gate_blk)),
            pl.BlockSpec((tm, d), lambda i: (i, gate_blk + 1)),
            pl.BlockSpec((tm, d), lambda i: (i, 0)),
            pl.BlockSpec((1, 1, d), lambda i: (i // per_b, 0, 0)),
            pl.BlockSpec((1, d), lambda i: (0, 0)),
            pl.BlockSpec((dc, d), lambda i: (0, 0), pipeline_mode=once),
            pl.BlockSpec((dc, d), lambda i: (0, 0), pipeline_mode=once),
            pl.BlockSpec((d, d), lambda i: (0, 0), pipeline_mode=once),
        ],
        out_specs=pl.BlockSpec((tm, d), lambda i: (i, 0)),
        compiler_params=_params(("arbitrary",)),
        name="merge",
    )(uc, oa, p, p, x2, gt1, g_post, wc, wa, wo)


FFN_ROWS = 512


def _ffn_kernel(x_ref, g_ref, sc_ref, sh_ref, gt_ref, gp_ref, wg_ref, wu_ref, wd_ref,
                o_ref, h_ref, acc_ref):
    j = pl.program_id(1)
    tm = x_ref.shape[0]
    n_rows = tm // FFN_ROWS

    @pl.when(j == 0)
    def _():
        g = g_ref[...]
        sc = sc_ref[0]
        sh = sh_ref[0]

        def body(r, carry):
            rows = pl.ds(pl.multiple_of(r * ROW_CHUNK, ROW_CHUNK), ROW_CHUNK)
            h_ref[rows, :] = _adaln_rows(x_ref[rows, :], g, sc, sh).astype(BF16)
            acc_ref[rows, :] = jnp.zeros((ROW_CHUNK, acc_ref.shape[1]), F32)
            return carry

        lax.fori_loop(0, tm // ROW_CHUNK, body, 0)

    for r in range(n_rows):
        rows = slice(r * FFN_ROWS, (r + 1) * FFN_ROWS)
        h = h_ref[rows, :]
        a = _dot(h, wg_ref[...])
        b = _dot(h, wu_ref[...])
        act = ((a * _sigmoid(a)) * b).astype(BF16)
        acc_ref[rows, :] = acc_ref[rows, :] + _dot(act, wd_ref[...])

    @pl.when(j == pl.num_programs(1) - 1)
    def _():
        gt = gt_ref[0]
        gp = gp_ref[...]

        def body(r, carry):
            rows = pl.ds(pl.multiple_of(r * ROW_CHUNK, ROW_CHUNK), ROW_CHUNK)
            o_ref[rows, :] = x_ref[rows, :] + gt * _rms_rows(acc_ref[rows, :], gp)
            return carry

        lax.fori_loop(0, tm // ROW_CHUNK, body, 0)


def _ffn(x2, g_pre, sc2, sh2, gt2, g_post, wg, wu, wd, seq, tm=512, tf=512):
    m, d = x2.shape
    dff = wg.shape[1]
    per_b = seq // tm
    return pl.pallas_call(
        _ffn_kernel,
        out_shape=jax.ShapeDtypeStruct((m, d), F32),
        grid=(m // tm, dff // tf),
        in_specs=[
            pl.BlockSpec((tm, d), lambda i, j: (i, 0)),
            pl.BlockSpec((1, d), lambda i, j: (0, 0)),
            pl.BlockSpec((1, 1, d), lambda i, j: (i // per_b, 0, 0)),
            pl.BlockSpec((1, 1, d), lambda i, j: (i // per_b, 0, 0)),
            pl.BlockSpec((1, 1, d), lambda i, j: (i // per_b, 0, 0)),
            pl.BlockSpec((1, d), lambda i, j: (0, 0)),
            pl.BlockSpec((d, tf), lambda i, j: (0, j)),
            pl.BlockSpec((d, tf), lambda i, j: (0, j)),
            pl.BlockSpec((tf, d), lambda i, j: (j, 0)),
        ],
        out_specs=pl.BlockSpec((tm, d), lambda i, j: (i, 0)),
        scratch_shapes=[pltpu.VMEM((tm, d), BF16), pltpu.VMEM((tm, d), F32)],
        compiler_params=_params(("arbitrary", "arbitrary")),
        name="ffn",
    )(x2, g_pre, sc2, sh2, gt2, g_post, wg, wu, wd)


def kernel(x, c, w_ada, b_ada, g_pre_mix, w_in, w_dw, b_dw, g_conv_ln, b_conv_ln, w_conv_out, w_attn_out, w_o, g_post_mix, g_pre_ffn, w_gate, w_up, w_down, g_post_ffn):
    batch, seq, d = x.shape
    depth = w_ada.shape[0]
    topk = min(TOPK_MAX, seq // 4)

    off_glu = 2 * D_CONV
    off_ki = off_glu + 3 * D_ATTN + IDX_HEADS * IDX_DIM
    off_wi = off_ki + IDX_DIM
    off_gate = off_wi + IDX_HEADS

    c8 = jnp.zeros((SUBLANES, d), F32).at[:batch].set(c)
    x2 = x.reshape(batch * seq, d)
    for l in range(depth):
        mod = _ada(c8, w_ada[l], b_ada[l][None, :])[:batch]
        sh1, sc1, gt1, sh2, sc2, gt2 = [t[:, None, :] for t in jnp.split(mod, 6, axis=-1)]

        w = w_in[l]
        w_bf = w.astype(BF16)
        w_gates = w_bf[:, off_gate:]
        w_ki = w_bf[:, off_ki:off_wi]
        w_wi = jnp.pad(w_bf[:, off_wi:off_gate], ((0, 0), (0, LANES - IDX_HEADS)))
        w_small = jnp.concatenate([w_ki, w_ki, w_wi], axis=1)

        p, kk, wi = _proj(x2, g_pre_mix[l][None, :], sc1, sh1, w_bf, w_gates, w_small, seq,
                          n_main_cols=off_ki)
        uc = _conv(p, w_dw[l], b_dw[l][None, :], g_conv_ln[l][None, :], b_conv_ln[l][None, :],
                   batch, seq)
        oa = _attn(p, kk, wi, batch, seq, topk)
        x2 = _merge(uc, oa, p, x2, gt1, g_post_mix[l][None, :],
                    w_conv_out[l].astype(BF16), w_attn_out[l].astype(BF16), w_o[l].astype(BF16), seq)
        x2 = _ffn(x2, g_pre_ffn[l][None, :], sc2, sh2, gt2, g_post_ffn[l][None, :],
                  w_gate[l].astype(BF16), w_up[l].astype(BF16), w_down[l].astype(BF16), seq)
    return x2.reshape(batch, seq, d)
```

```python
import functools

import jax
import jax.numpy as jnp
from jax import lax
from jax.experimental import pallas as pl
from jax.experimental.pallas import tpu as pltpu

F32 = jnp.float32
BF16 = jnp.bfloat16
I32 = jnp.int32

CHUNK = 64
CHUNK_SHIFT = 6
D_CONV = 1024
CONV_WIDTH = 31
N_HEADS = 16
HEAD_DIM = 64
D_ATTN = N_HEADS * HEAD_DIM
IDX_HEADS = 16
IDX_DIM = 64
TOPK_MAX = 256
EPS = 1e-6

LANES = 128
SUBLANES = 8
VMEM_LIMIT_BYTES = 56 * 1024 * 1024

LOG2_E = 1.4426950408889634
NEG_BIG = -1e30
INT_MIN = -(2 ** 31)
NEG_INF_KEY = (0xFF800000 ^ 0x7FFFFFFF) - 2 ** 32


def _sigmoid(x):
    return 1.0 / (1.0 + jnp.exp(-x))


def _dot(a, b):
    return jnp.dot(a, b, preferred_element_type=F32)


def _dot_nt(a, b):
    return lax.dot_general(a, b, (((1,), (1,)), ((), ())), preferred_element_type=F32)


def _params(sem):
    return pltpu.CompilerParams(dimension_semantics=sem, vmem_limit_bytes=VMEM_LIMIT_BYTES)


def _ada_kernel(c_ref, w_ref, b_ref, o_ref):
    c = c_ref[...]
    ca = (c * _sigmoid(c)).astype(BF16)
    o_ref[...] = _dot(ca, w_ref[...].astype(BF16)) + b_ref[...]


def _ada(c8, w, b, tn=1024):
    rows, d = c8.shape
    n = w.shape[1]
    return pl.pallas_call(
        _ada_kernel,
        out_shape=jax.ShapeDtypeStruct((rows, n), F32),
        grid=(n // tn,),
        in_specs=[
            pl.BlockSpec((rows, d), lambda j: (0, 0)),
            pl.BlockSpec((d, tn), lambda j: (0, j)),
            pl.BlockSpec((1, tn), lambda j: (0, j)),
        ],
        out_specs=pl.BlockSpec((rows, tn), lambda j: (0, j)),
        compiler_params=_params(("arbitrary",)),
        name="ada",
    )(c8, w, b)


ROW_CHUNK = 128


def _adaln_rows(x, g, sc, sh):
    ms = jnp.mean(x * x, axis=-1, keepdims=True)
    y = (x * lax.rsqrt(ms + EPS)) * g
    return y * (1.0 + sc) + sh


def _proj_kernel(x_ref, g_ref, sc_ref, sh_ref, w_ref, wg_ref, ws_ref, o_ref, kk_ref, wi_ref, h_ref,
                 *, n_main):
    j = pl.program_id(1)
    tm = x_ref.shape[0]

    @pl.when(j == 0)
    def _():
        g = g_ref[...]
        sc = sc_ref[0]
        sh = sh_ref[0]

        def body(r, carry):
            r0 = pl.multiple_of(r * ROW_CHUNK, ROW_CHUNK)
            h = _adaln_rows(x_ref[pl.ds(r0, ROW_CHUNK), :], g, sc, sh).astype(BF16)
            h_ref[pl.ds(r0, ROW_CHUNK), :] = h
            small = _dot(h, ws_ref[...])
            kk_ref[pl.ds(r0, ROW_CHUNK), :] = small[:, :LANES].astype(BF16)
            wi_ref[pl.ds(r0, ROW_CHUNK), :] = small[:, LANES:]
            return carry

        lax.fori_loop(0, tm // ROW_CHUNK, body, 0)

    @pl.when(j < n_main)
    def _():
        o_ref[...] = _dot(h_ref[...], w_ref[...]).astype(BF16)

    @pl.when(j >= n_main)
    def _():
        o_ref[...] = _dot(h_ref[...], wg_ref[...]).astype(BF16)


def _proj(x2, g, sc, sh, w_main, w_gates, w_small, seq, n_main_cols, tm=1024, tn=1024):
    m, d = x2.shape
    n_main = n_main_cols // tn
    n = n_main_cols + w_gates.shape[1]
    per_b = seq // tm
    return pl.pallas_call(
        functools.partial(_proj_kernel, n_main=n_main),
        out_shape=(
            jax.ShapeDtypeStruct((m, n), BF16),
            jax.ShapeDtypeStruct((m, LANES), BF16),
            jax.ShapeDtypeStruct((m, LANES), F32),
        ),
        grid=(m // tm, n // tn),
        in_specs=[
            pl.BlockSpec((tm, d), lambda i, j: (i, 0)),
            pl.BlockSpec((1, d), lambda i, j: (0, 0)),
            pl.BlockSpec((1, 1, d), lambda i, j: (i // per_b, 0, 0)),
            pl.BlockSpec((1, 1, d), lambda i, j: (i // per_b, 0, 0)),
            pl.BlockSpec((d, tn), lambda i, j: (0, jnp.minimum(j, n_main - 1))),
            pl.BlockSpec((d, tn), lambda i, j: (0, jnp.maximum(j - n_main, 0))),
            pl.BlockSpec((d, 2 * LANES), lambda i, j: (0, 0)),
        ],
        out_specs=(
            pl.BlockSpec((tm, tn), lambda i, j: (i, j)),
            pl.BlockSpec((tm, LANES), lambda i, j: (i, 0)),
            pl.BlockSpec((tm, LANES), lambda i, j: (i, 0)),
        ),
        scratch_shapes=[pltpu.VMEM((tm, d), BF16)],
        compiler_params=_params(("arbitrary", "arbitrary")),
        name="proj",
    )(x2, g, sc, sh, w_main, w_gates, w_small)


CONV_HALO = 32
CONV_ROWS = 64
NORM_ROWS = 256


def _conv_kernel(a_ref, gt_ref, w_ref, bdw_ref, gln_ref, bln_ref, *rest):
    n_cast = (len(rest) - 5) // 2
    o_ref = rest[n_cast]
    u_ref, cv_ref, wb_ref, sh_ref = rest[2 * n_cast + 1:]
    _cast_slabs(rest[:n_cast] + rest[n_cast + 1:2 * n_cast + 1])

    s = pl.program_id(1)
    ts = a_ref.shape[0]
    dc = a_ref.shape[1]

    @pl.when(s == 0)
    def _():
        u_ref[0:CONV_HALO, :] = jnp.zeros((CONV_HALO, dc), F32)

    @pl.when(s > 0)
    def _():
        u_ref[0:CONV_HALO, :] = u_ref[ts:ts + CONV_HALO, :]

    def glu_body(r, carry):
        r0 = pl.multiple_of(r * CONV_ROWS, CONV_ROWS)
        a = a_ref[pl.ds(r0, CONV_ROWS), :].astype(F32)
        g = gt_ref[pl.ds(r0, CONV_ROWS), :].astype(F32)
        u_ref[pl.ds(CONV_HALO + r0, CONV_ROWS), :] = a * _sigmoid(g)
        return carry

    lax.fori_loop(0, ts // CONV_ROWS, glu_body, 0)

    for jtap in range(CONV_WIDTH):
        wb_ref[jtap * SUBLANES:(jtap + 1) * SUBLANES, :] = jnp.broadcast_to(
            w_ref[jtap:jtap + 1, :], (SUBLANES, dc))

    base = CONV_HALO - (CONV_WIDTH - 1)

    taps = {}
    for jtap in range(CONV_WIDTH):
        a, b = divmod(base + jtap, SUBLANES)
        taps.setdefault(b, []).append((a, jtap))

    def conv_body(r, carry):
        r0 = pl.multiple_of(r * CONV_ROWS, CONV_ROWS)
        for lc in range(dc // LANES):
            cols = slice(lc * LANES, (lc + 1) * LANES)
            win = u_ref[pl.ds(r0, CONV_ROWS + CONV_HALO), cols]
            for b, group in taps.items():
                span = CONV_ROWS + SUBLANES * max(a for a, _ in group)
                sh_ref[lc, b, 0:span, :] = win[b:b + span, :]
            acc = jnp.zeros((CONV_ROWS, LANES), F32)
            for b, group in taps.items():
                for a, jtap in group:
                    w_tap = wb_ref[jtap * SUBLANES:(jtap + 1) * SUBLANES, cols]
                    w_rows = jnp.concatenate([w_tap] * (CONV_ROWS // SUBLANES), axis=0)
                    acc = acc + sh_ref[lc, b, SUBLANES * a:SUBLANES * a + CONV_ROWS, :] * w_rows
            cv_ref[pl.ds(r0, CONV_ROWS), cols] = acc + bdw_ref[:, cols]
        return carry

    lax.fori_loop(0, ts // CONV_ROWS, conv_body, 0)

    def norm_body(r, carry):
        r0 = pl.multiple_of(r * NORM_ROWS, NORM_ROWS)
        xr = cv_ref[pl.ds(r0, NORM_ROWS), :]
        mu = jnp.mean(xr, axis=-1, keepdims=True)
        xc = xr - mu
        var = jnp.mean(xc * xc, axis=-1, keepdims=True)
        y = (xc * lax.rsqrt(var + EPS)) * gln_ref[...] + bln_ref[...]
        o_ref[pl.ds(r0, NORM_ROWS), :] = (y * _sigmoid(y)).astype(BF16)
        return carry

    lax.fori_loop(0, ts // NORM_ROWS, norm_body, 0)


def _conv(p, w_dw, b_dw, g_ln, b_ln, later_weights, batch, seq, ts=512):
    m = p.shape[0]
    dc = w_dw.shape[1]
    per_b = seq // ts
    cast_in, cast_out, cast_shapes = _cast_specs(
        later_weights, batch * per_b, lambda b, s: (b * per_b + s, 0))
    return pl.pallas_call(
        _conv_kernel,
        out_shape=[jax.ShapeDtypeStruct((m, dc), BF16)] + cast_shapes,
        grid=(batch, per_b),
        in_specs=[
            pl.BlockSpec((ts, dc), lambda b, s: (b * per_b + s, 0)),
            pl.BlockSpec((ts, dc), lambda b, s: (b * per_b + s, 1)),
            pl.BlockSpec((CONV_WIDTH, dc), lambda b, s: (0, 0)),
            pl.BlockSpec((1, dc), lambda b, s: (0, 0)),
            pl.BlockSpec((1, dc), lambda b, s: (0, 0)),
            pl.BlockSpec((1, dc), lambda b, s: (0, 0)),
        ] + cast_in,
        out_specs=[pl.BlockSpec((ts, dc), lambda b, s: (b * per_b + s, 0))] + cast_out,
        scratch_shapes=[
            pltpu.VMEM((ts + CONV_HALO, dc), F32),
            pltpu.VMEM((ts, dc), F32),
            pltpu.VMEM((CONV_WIDTH * SUBLANES, dc), F32),
            pltpu.VMEM((dc // LANES, SUBLANES, CONV_ROWS + CONV_HALO, LANES), F32),
        ],
        compiler_params=_params(("arbitrary", "arbitrary")),
        name="conv",
    )(p, p, w_dw, b_dw, g_ln, b_ln, *later_weights)


Q_TILE = 256
Q_SUB = 128
Q_SUBS = Q_TILE // Q_SUB
KEY_TILE = 256
KEY_SUB = 256
SEL_ROWS = 64
LANE_CHUNKS = KEY_TILE // LANES
INDEX_BITS = 13


def _fold_lanes(acc, t, op):
    for ch in range(t.shape[1] // LANES):
        acc = op(acc, t[:, ch * LANES:(ch + 1) * LANES])
    return acc


def _attn_kernel(qi_ref, q_ref, k_ref, v_ref, kk_ref, wi_ref, o_ref,
                 qim_ref, qm_ref, wrep_ref, key_ref, bias_ref, m_ref, acc_ref, thr_ref,
                 cut_ref, cand_ref, idx_ref, *, topk):
    i = pl.program_id(1)
    p0 = i * Q_TILE
    n_kt = (p0 + Q_TILE + KEY_TILE - 1) // KEY_TILE
    idx_scale = (IDX_DIM ** -0.5) * (IDX_HEADS ** -0.5)
    attn_scale = HEAD_DIM ** -0.5

    lane = lax.broadcasted_iota(I32, (Q_TILE, LANES), 1)
    lo = lane < HEAD_DIM
    lo_sub = lax.broadcasted_iota(I32, (Q_SUB, LANES), 1) < HEAD_DIM
    row = lax.broadcasted_iota(I32, (Q_TILE, KEY_TILE), 0)
    col = lax.broadcasted_iota(I32, (Q_TILE, KEY_TILE), 1)
    q_chunk = (p0 + row) >> CHUNK_SHIFT

    wi = wi_ref[...]
    for hp in range(N_HEADS // 2):
        cols = slice(hp * LANES, (hp + 1) * LANES)
        q_p = q_ref[:, cols].astype(F32) * (attn_scale * LOG2_E)
        qm_ref[2 * hp] = jnp.where(lo, q_p, 0.0).astype(BF16)
        qm_ref[2 * hp + 1] = jnp.where(lo, 0.0, q_p).astype(BF16)
        for rs in range(Q_SUBS):
            qi_p = qi_ref[rs * Q_SUB:(rs + 1) * Q_SUB, cols]
            zero_i = jnp.zeros_like(qi_p)
            qim_ref[rs, (2 * hp) * Q_SUB:(2 * hp + 1) * Q_SUB, :] = jnp.where(lo_sub, qi_p, zero_i)
            qim_ref[rs, (2 * hp + 1) * Q_SUB:(2 * hp + 2) * Q_SUB, :] = jnp.where(lo_sub, zero_i, qi_p)
    for h in range(IDX_HEADS):
        wrep_ref[h] = jnp.broadcast_to(wi[:, h:h + 1], (Q_TILE, LANES))

    sub_row = lax.broadcasted_iota(I32, (Q_SUB, KEY_SUB), 0)
    sub_col = lax.broadcasted_iota(I32, (Q_SUB, KEY_SUB), 1)

    def score_body(kt, carry):
        k0 = pl.multiple_of(kt * KEY_TILE, KEY_TILE)
        for ks in range(KEY_TILE // KEY_SUB):
            kk_s = kk_ref[pl.ds(k0 + ks * KEY_SUB, KEY_SUB), :]
            for rs in range(Q_SUBS):
                rows = slice(rs * Q_SUB, (rs + 1) * Q_SUB)
                rel = _dot_nt(qim_ref[rs], kk_s)
                acc = [jnp.zeros((Q_SUB, LANES), F32) for _ in range(KEY_SUB // LANES)]
                for h in range(IDX_HEADS):
                    w_h = wrep_ref[h, rows, :]
                    for ch in range(KEY_SUB // LANES):
                        r = rel[h * Q_SUB:(h + 1) * Q_SUB, ch * LANES:(ch + 1) * LANES]
                        acc[ch] = acc[ch] + jnp.maximum(r, 0.0) * w_h
                score = jnp.concatenate(acc, axis=-1) * idx_scale
                adm = ((k0 + ks * KEY_SUB + sub_col) >> CHUNK_SHIFT) <= (
                    (p0 + rs * Q_SUB + sub_row) >> CHUNK_SHIFT)
                key_ref[kt, rows, ks * KEY_SUB:(ks + 1) * KEY_SUB] = jnp.where(adm, score, -jnp.inf)
        return carry

    lax.fori_loop(0, n_kt, score_body, 0)

    kf = float(topk)

    def key_to_float(key_s):
        bits = key_s ^ ((key_s >> 31) & 0x7FFFFFFF)
        return jnp.where(key_s <= NEG_INF_KEY, -jnp.inf, pltpu.bitcast(bits, F32))

    lane_s = lax.broadcasted_iota(I32, (SEL_ROWS, LANES), 1)

    row_groups = [slice(rg * SEL_ROWS, (rg + 1) * SEL_ROWS) for rg in range(Q_TILE // SEL_ROWS)]

    def count(indicator):
        def body(kt, cs):
            out = []
            for rows, c in zip(row_groups, cs):
                for ch in range(LANE_CHUNKS):
                    kc = key_ref[kt, rows, ch * LANES:(ch + 1) * LANES]
                    c = c + indicator(kc, kt * KEY_TILE + ch * LANES + lane_s, rows)
                out.append(c)
            return tuple(out)

        zero = jnp.zeros((SEL_ROWS, LANES), F32)
        cs = lax.fori_loop(0, n_kt, body, tuple(zero for _ in row_groups))
        return jnp.sum(jnp.concatenate(cs, axis=0), axis=-1, keepdims=True)

    def bisect_body(it, prefix):
        bit = lax.shift_left(jnp.int32(1), jnp.int32(31) - it)
        cand_u = prefix | bit
        cand_ref[...] = key_to_float(cand_u ^ INT_MIN)
        cnt = count(lambda kc, _, rows: jnp.where(kc >= cand_ref[rows, :], 1.0, 0.0))
        return jnp.where(cnt >= kf, cand_u, prefix)

    prefix = lax.fori_loop(0, 32, bisect_body, jnp.zeros((Q_TILE, LANES), I32))
    thr = key_to_float(prefix ^ INT_MIN)
    thr_ref[...] = thr

    n_gt = count(lambda kc, _, rows: jnp.where(kc > thr_ref[rows, :], 1.0, 0.0))
    n_ge = count(lambda kc, _, rows: jnp.where(kc >= thr_ref[rows, :], 1.0, 0.0))
    want = kf - n_gt
    cut_ref[...] = jnp.full((Q_TILE, LANES), 2 ** INDEX_BITS, I32)
    tied = jnp.where(thr[:, :1] == -jnp.inf, 0.0, jnp.where(n_ge > kf, 1.0, 0.0))
    surplus = jnp.max(tied)

    @pl.when(surplus > 0.0)
    def _():
        def cut_body(it, cut):
            bit = lax.shift_left(jnp.int32(1), jnp.int32(INDEX_BITS - 1) - it)
            cand = cut | bit
            idx_ref[...] = cand
            cnt = count(lambda kc, idx, rows: jnp.where(
                kc == thr_ref[rows, :], jnp.where(idx < idx_ref[rows, :], 1.0, 0.0), 0.0))
            return jnp.where(cnt <= want, cand, cut)

        cut_ref[...] = lax.fori_loop(0, INDEX_BITS, cut_body, jnp.zeros((Q_TILE, LANES), I32))

    cut = cut_ref[...]

    def bias_body(kt, carry):
        k0 = pl.multiple_of(kt * KEY_TILE, KEY_TILE)
        kt_keys = key_ref[kt]
        adm = ((k0 + col) >> CHUNK_SHIFT) <= q_chunk
        pieces = []
        for ch in range(LANE_CHUNKS):
            kc = kt_keys[:, ch * LANES:(ch + 1) * LANES]
            idx = k0 + ch * LANES + lane
            tie = jnp.where(kc == thr, jnp.where(idx < cut, 0.0, -jnp.inf), -jnp.inf)
            pieces.append(jnp.where(kc > thr, 0.0, tie))
        b = jnp.concatenate(pieces, axis=-1)
        bias_ref[kt] = jnp.where(adm, b, -jnp.inf)
        return carry

    lax.fori_loop(0, n_kt, bias_body, 0)

    for h in range(N_HEADS):
        m_ref[h] = jnp.full((Q_TILE, LANES), NEG_BIG, F32)
        acc_ref[h] = jnp.zeros((Q_TILE, LANES), F32)

    lo_kv = lax.broadcasted_iota(I32, (KEY_TILE, LANES), 1) < HEAD_DIM

    def attn_body(kt, carry):
        k0 = pl.multiple_of(kt * KEY_TILE, KEY_TILE)
        for hp in range(N_HEADS // 2):
            cols = slice(hp * LANES, (hp + 1) * LANES)
            k_t = k_ref[pl.ds(k0, KEY_TILE), cols]
            v_t = v_ref[pl.ds(k0, KEY_TILE), cols]
            one = jnp.ones_like(v_t)
            v_ext = (jnp.where(lo_kv, v_t, one), jnp.where(lo_kv, one, v_t))
            for par in range(2):
                h = 2 * hp + par
                s = _dot_nt(qm_ref[h], k_t) + bias_ref[kt]
                t_max = jnp.max(_fold_lanes(jnp.full((Q_TILE, LANES), -jnp.inf, F32), s, jnp.maximum),
                                axis=-1, keepdims=True)
                m_old = m_ref[h]
                m_new = jnp.maximum(m_old, t_max)
                alpha = jnp.exp2(m_old - m_new)
                e = jnp.concatenate(
                    [jnp.exp2(s[:, ch * LANES:(ch + 1) * LANES] - m_new) for ch in range(LANE_CHUNKS)],
                    axis=-1)
                acc_ref[h] = alpha * acc_ref[h] + _dot(e.astype(BF16), v_ext[par])
                m_ref[h] = m_new
        return carry

    lax.fori_loop(0, n_kt, attn_body, 0)

    for hp in range(N_HEADS // 2):
        outs = []
        for h in (2 * hp, 2 * hp + 1):
            acc = acc_ref[h]
            outs.append(acc / pltpu.roll(acc, HEAD_DIM, 1))
        o_ref[:, hp * LANES:(hp + 1) * LANES] = jnp.where(lo, outs[0], outs[1]).astype(BF16)


def _attn(p, kk, wi, batch, seq, topk):
    m = p.shape[0]
    nq = seq // Q_TILE
    n_kt = seq // KEY_TILE
    once = pl.Buffered(1)
    return pl.pallas_call(
        functools.partial(_attn_kernel, topk=topk),
        out_shape=jax.ShapeDtypeStruct((m, D_ATTN), BF16),
        grid=(batch, nq),
        in_specs=[
            pl.BlockSpec((Q_TILE, D_ATTN), lambda b, i: (b * nq + i, 5)),
            pl.BlockSpec((Q_TILE, D_ATTN), lambda b, i: (b * nq + i, 2)),
            pl.BlockSpec((seq, D_ATTN), lambda b, i: (b, 3), pipeline_mode=once),
            pl.BlockSpec((seq, D_ATTN), lambda b, i: (b, 4), pipeline_mode=once),
            pl.BlockSpec((seq, LANES), lambda b, i: (b, 0), pipeline_mode=once),
            pl.BlockSpec((Q_TILE, LANES), lambda b, i: (b * nq + i, 0)),
        ],
        out_specs=pl.BlockSpec((Q_TILE, D_ATTN), lambda b, i: (b * nq + i, 0)),
        scratch_shapes=[
            pltpu.VMEM((Q_SUBS, IDX_HEADS * Q_SUB, LANES), BF16),
            pltpu.VMEM((N_HEADS, Q_TILE, LANES), BF16),
            pltpu.VMEM((IDX_HEADS, Q_TILE, LANES), F32),
            pltpu.VMEM((n_kt, Q_TILE, KEY_TILE), F32),
            pltpu.VMEM((n_kt, Q_TILE, KEY_TILE), F32),
            pltpu.VMEM((N_HEADS, Q_TILE, LANES), F32),
            pltpu.VMEM((N_HEADS, Q_TILE, LANES), F32),
            pltpu.VMEM((Q_TILE, LANES), F32),
            pltpu.VMEM((Q_TILE, LANES), I32),
            pltpu.VMEM((Q_TILE, LANES), F32),
            pltpu.VMEM((Q_TILE, LANES), I32),
        ],
        compiler_params=_params(("arbitrary", "arbitrary")),
        name="attn",
    )(p, p, p, p, kk, wi)


MERGE_ROWS = 256


def _rms_rows(x, g):
    ms = jnp.mean(x * x, axis=-1, keepdims=True)
    return (x * lax.rsqrt(ms + EPS)) * g


def _merge_kernel(uc_ref, oa_ref, gc_ref, ga_ref, x_ref, gt_ref, gp_ref,
                  wc_ref, wa_ref, wo_ref, *rest):
    n_cast = (len(rest) - 1) // 2
    o_ref = rest[n_cast]
    _cast_slabs(rest[:n_cast] + rest[n_cast + 1:])

    tm = x_ref.shape[0]
    gt = gt_ref[0]
    gp = gp_ref[...]

    def body(r, carry):
        r0 = pl.multiple_of(r * MERGE_ROWS, MERGE_ROWS)
        rows = pl.ds(r0, MERGE_ROWS)
        yc = _dot(uc_ref[rows, :], wc_ref[...])
        ya = _dot(oa_ref[rows, :], wa_ref[...])
        gc = _sigmoid(gc_ref[rows, :].astype(F32))
        ga = _sigmoid(ga_ref[rows, :].astype(F32))
        merged = (gc * yc + ga * ya).astype(BF16)
        mixed = _dot(merged, wo_ref[...])
        o_ref[rows, :] = x_ref[rows, :] + gt * _rms_rows(mixed, gp)
        return carry

    lax.fori_loop(0, tm // MERGE_ROWS, body, 0)


def _cast_specs(weights, steps, index_map):
    in_specs, out_specs, out_shapes = [], [], []
    for w in weights:
        rows, cols = w.shape
        slab = rows // steps
        assert slab * steps == rows and slab % (2 * SUBLANES) == 0, (w.shape, steps)
        in_specs.append(pl.BlockSpec((slab, cols), index_map))
        out_specs.append(pl.BlockSpec((slab, cols), index_map))
        out_shapes.append(jax.ShapeDtypeStruct((rows, cols), BF16))
    return in_specs, out_specs, out_shapes


def _cast_slabs(refs):
    half = len(refs) // 2
    for src, dst in zip(refs[:half], refs[half:]):
        dst[...] = src[...].astype(BF16)


def _merge(uc, oa, p, x2, gt1, g_post, wc, wa, wo, next_weights, seq, tm=256):
    m, d = x2.shape
    dc = uc.shape[1]
    per_b = seq // tm
    gate_blk = p.shape[1] // d - 2
    once = pl.Buffered(1)
    cast_in, cast_out, cast_shapes = _cast_specs(next_weights, m // tm, lambda i: (i, 0))
    return pl.pallas_call(
        _merge_kernel,
        out_shape=[jax.ShapeDtypeStruct((m, d), F32)] + cast_shapes,
        grid=(m // tm,),
        in_specs=[
            pl.BlockSpec((tm, dc), lambda i: (i, 0)),
            pl.BlockSpec((tm, dc), lambda i: (i, 0)),
            pl.BlockSpec((tm, d), lambda i: (i, gate_blk)),
            pl.BlockSpec((tm, d), lambda i: (i, gate_blk + 1)),
            pl.BlockSpec((tm, d), lambda i: (i, 0)),
            pl.BlockSpec((1, 1, d), lambda i: (i // per_b, 0, 0)),
            pl.BlockSpec((1, d), lambda i: (0, 0)),
            pl.BlockSpec((dc, d), lambda i: (0, 0), pipeline_mode=once),
            pl.BlockSpec((dc, d), lambda i: (0, 0), pipeline_mode=once),
            pl.BlockSpec((d, d), lambda i: (0, 0), pipeline_mode=once),
        ] + cast_in,
        out_specs=[pl.BlockSpec((tm, d), lambda i: (i, 0))] + cast_out,
        compiler_params=_params(("arbitrary",)),
        name="merge",
    )(uc, oa, p, p, x2, gt1, g_post, wc, wa, wo, *next_weights)


FFN_ROWS = 512


def _ffn_kernel(x_ref, g_ref, sc_ref, sh_ref, gt_ref, gp_ref, wg_ref, wu_ref, wd_ref,
                o_ref, h_ref, acc_ref):
    j = pl.program_id(1)
    tm = x_ref.shape[0]
    n_rows = tm // FFN_ROWS

    @pl.when(j == 0)
    def _():
        g = g_ref[...]
        sc = sc_ref[0]
        sh = sh_ref[0]

        def body(r, carry):
            rows = pl.ds(pl.multiple_of(r * ROW_CHUNK, ROW_CHUNK), ROW_CHUNK)
            h_ref[rows, :] = _adaln_rows(x_ref[rows, :], g, sc, sh).astype(BF16)
            acc_ref[rows, :] = jnp.zeros((ROW_CHUNK, acc_ref.shape[1]), F32)
            return carry

        lax.fori_loop(0, tm // ROW_CHUNK, body, 0)

    for r in range(n_rows):
        rows = slice(r * FFN_ROWS, (r + 1) * FFN_ROWS)
        h = h_ref[rows, :]
        a = _dot(h, wg_ref[...])
        b = _dot(h, wu_ref[...])
        act = ((a * _sigmoid(a)) * b).astype(BF16)
        acc_ref[rows, :] = acc_ref[rows, :] + _dot(act, wd_ref[...])

    @pl.when(j == pl.num_programs(1) - 1)
    def _():
        gt = gt_ref[0]
        gp = gp_ref[...]

        def body(r, carry):
            rows = pl.ds(pl.multiple_of(r * ROW_CHUNK, ROW_CHUNK), ROW_CHUNK)
            o_ref[rows, :] = x_ref[rows, :] + gt * _rms_rows(acc_ref[rows, :], gp)
            return carry

        lax.fori_loop(0, tm // ROW_CHUNK, body, 0)


def _ffn(x2, g_pre, sc2, sh2, gt2, g_post, wg, wu, wd, seq, tm=512, tf=512):
    m, d = x2.shape
    dff = wg.shape[1]
    per_b = seq // tm
    return pl.pallas_call(
        _ffn_kernel,
        out_shape=jax.ShapeDtypeStruct((m, d), F32),
        grid=(m // tm, dff // tf),
        in_specs=[
            pl.BlockSpec((tm, d), lambda i, j: (i, 0)),
            pl.BlockSpec((1, d), lambda i, j: (0, 0)),
            pl.BlockSpec((1, 1, d), lambda i, j: (i // per_b, 0, 0)),
            pl.BlockSpec((1, 1, d), lambda i, j: (i // per_b, 0, 0)),
            pl.BlockSpec((1, 1, d), lambda i, j: (i // per_b, 0, 0)),
            pl.BlockSpec((1, d), lambda i, j: (0, 0)),
            pl.BlockSpec((d, tf), lambda i, j: (0, j)),
            pl.BlockSpec((d, tf), lambda i, j: (0, j)),
            pl.BlockSpec((tf, d), lambda i, j: (j, 0)),
        ],
        out_specs=pl.BlockSpec((tm, d), lambda i, j: (i, 0)),
        scratch_shapes=[pltpu.VMEM((tm, d), BF16), pltpu.VMEM((tm, d), F32)],
        compiler_params=_params(("arbitrary", "arbitrary")),
        name="ffn",
    )(x2, g_pre, sc2, sh2, gt2, g_post, wg, wu, wd)


def kernel(x, c, w_ada, b_ada, g_pre_mix, w_in, w_dw, b_dw, g_conv_ln, b_conv_ln, w_conv_out, w_attn_out, w_o, g_post_mix, g_pre_ffn, w_gate, w_up, w_down, g_post_ffn):
    batch, seq, d = x.shape
    depth = w_ada.shape[0]
    topk = min(TOPK_MAX, seq // 4)

    off_glu = 2 * D_CONV
    off_ki = off_glu + 3 * D_ATTN + IDX_HEADS * IDX_DIM
    off_wi = off_ki + IDX_DIM
    off_gate = off_wi + IDX_HEADS

    c8 = jnp.zeros((SUBLANES, d), F32).at[:batch].set(c)
    x2 = x.reshape(batch * seq, d)
    for l in range(depth):
        mod = _ada(c8, w_ada[l], b_ada[l][None, :])[:batch]
        sh1, sc1, gt1, sh2, sc2, gt2 = [t[:, None, :] for t in jnp.split(mod, 6, axis=-1)]

        w = w_in[l]
        w_bf = w.astype(BF16)
        w_gates = w_bf[:, off_gate:]
        w_ki = w_bf[:, off_ki:off_wi]
        w_wi = jnp.pad(w_bf[:, off_wi:off_gate], ((0, 0), (0, LANES - IDX_HEADS)))
        w_small = jnp.concatenate([w_ki, w_ki, w_wi], axis=1)

        p, kk, wi = _proj(x2, g_pre_mix[l][None, :], sc1, sh1, w_bf, w_gates, w_small, seq,
                          n_main_cols=off_ki)
        uc, wc_bf, wa_bf, wo_bf = _conv(
            p, w_dw[l], b_dw[l][None, :], g_conv_ln[l][None, :], b_conv_ln[l][None, :],
            (w_conv_out[l], w_attn_out[l], w_o[l]), batch, seq)
        oa = _attn(p, kk, wi, batch, seq, topk)
        x2, wg_bf, wu_bf, wd_bf = _merge(
            uc, oa, p, x2, gt1, g_post_mix[l][None, :], wc_bf, wa_bf, wo_bf,
            (w_gate[l], w_up[l], w_down[l]), seq)
        x2 = _ffn(x2, g_pre_ffn[l][None, :], sc2, sh2, gt2, g_post_ffn[l][None, :],
                  wg_bf, wu_bf, wd_bf, seq)
    return x2.reshape(batch, seq, d)
```

```python
import functools

import jax
import jax.numpy as jnp
from jax import lax
from jax.experimental import pallas as pl
from jax.experimental.pallas import tpu as pltpu

F32 = jnp.float32
BF16 = jnp.bfloat16
I32 = jnp.int32

CHUNK = 64
CHUNK_SHIFT = 6
D_CONV = 1024
CONV_WIDTH = 31
N_HEADS = 16
HEAD_DIM = 64
D_ATTN = N_HEADS * HEAD_DIM
IDX_HEADS = 16
IDX_DIM = 64
TOPK_MAX = 256
EPS = 1e-6

LANES = 128
SUBLANES = 8
VMEM_LIMIT_BYTES = 56 * 1024 * 1024

LOG2_E = 1.4426950408889634
NEG_BIG = -1e30
INT_MIN = -(2 ** 31)
NEG_INF_KEY = (0xFF800000 ^ 0x7FFFFFFF) - 2 ** 32


def _sigmoid(x):
    return 1.0 / (1.0 + jnp.exp(-x))


def _dot(a, b):
    return jnp.dot(a, b, preferred_element_type=F32)


def _dot_nt(a, b):
    return lax.dot_general(a, b, (((1,), (1,)), ((), ())), preferred_element_type=F32)


def _params(sem):
    return pltpu.CompilerParams(dimension_semantics=sem, vmem_limit_bytes=VMEM_LIMIT_BYTES)


def _ada_kernel(c_ref, w_ref, b_ref, *rest):
    n_cast = (len(rest) - 1) // 2
    o_ref = rest[n_cast]
    _cast_slabs(rest[:n_cast] + rest[n_cast + 1:])
    c = c_ref[...]
    ca = (c * _sigmoid(c)).astype(BF16)
    o_ref[...] = _dot(ca, w_ref[...].astype(BF16)) + b_ref[...]


def _ada(c8, w, b, later_weights, tn=768):
    rows, d = c8.shape
    n = w.shape[1]
    cast_in, cast_out, cast_shapes = _cast_specs(later_weights, n // tn, lambda j: (j, 0))
    return pl.pallas_call(
        _ada_kernel,
        out_shape=[jax.ShapeDtypeStruct((rows, n), F32)] + cast_shapes,
        grid=(n // tn,),
        in_specs=[
            pl.BlockSpec((rows, d), lambda j: (0, 0)),
            pl.BlockSpec((d, tn), lambda j: (0, j)),
            pl.BlockSpec((1, tn), lambda j: (0, j)),
        ] + cast_in,
        out_specs=[pl.BlockSpec((rows, tn), lambda j: (0, j))] + cast_out,
        compiler_params=_params(("arbitrary",)),
        name="ada",
    )(c8, w, b, *later_weights)


ROW_CHUNK = 128


def _adaln_rows(x, g, sc, sh):
    ms = jnp.mean(x * x, axis=-1, keepdims=True)
    y = (x * lax.rsqrt(ms + EPS)) * g
    return y * (1.0 + sc) + sh


def _proj_kernel(x_ref, g_ref, sc_ref, sh_ref, w_ref, wg_ref, ws_ref, o_ref, kk_ref, wi_ref, h_ref,
                 *, n_main):
    j = pl.program_id(1)
    tm = x_ref.shape[0]

    @pl.when(j == 0)
    def _():
        g = g_ref[...]
        sc = sc_ref[0]
        sh = sh_ref[0]

        def body(r, carry):
            r0 = pl.multiple_of(r * ROW_CHUNK, ROW_CHUNK)
            h = _adaln_rows(x_ref[pl.ds(r0, ROW_CHUNK), :], g, sc, sh).astype(BF16)
            h_ref[pl.ds(r0, ROW_CHUNK), :] = h
            small = _dot(h, ws_ref[...])
            kk_ref[pl.ds(r0, ROW_CHUNK), :] = small[:, :LANES].astype(BF16)
            wi_ref[pl.ds(r0, ROW_CHUNK), :] = small[:, LANES:]
            return carry

        lax.fori_loop(0, tm // ROW_CHUNK, body, 0)

    @pl.when(j < n_main)
    def _():
        o_ref[...] = _dot(h_ref[...], w_ref[...]).astype(BF16)

    @pl.when(j >= n_main)
    def _():
        o_ref[...] = _dot(h_ref[...], wg_ref[...]).astype(BF16)


def _proj(x2, g, sc, sh, w_main, w_gates, w_small, seq, n_main_cols, tm=1024, tn=1024):
    m, d = x2.shape
    n_main = n_main_cols // tn
    n = n_main_cols + w_gates.shape[1]
    per_b = seq // tm
    return pl.pallas_call(
        functools.partial(_proj_kernel, n_main=n_main),
        out_shape=(
            jax.ShapeDtypeStruct((m, n), BF16),
            jax.ShapeDtypeStruct((m, LANES), BF16),
            jax.ShapeDtypeStruct((m, LANES), F32),
        ),
        grid=(m // tm, n // tn),
        in_specs=[
            pl.BlockSpec((tm, d), lambda i, j: (i, 0)),
            pl.BlockSpec((1, d), lambda i, j: (0, 0)),
            pl.BlockSpec((1, 1, d), lambda i, j: (i // per_b, 0, 0)),
            pl.BlockSpec((1, 1, d), lambda i, j: (i // per_b, 0, 0)),
            pl.BlockSpec((d, tn), lambda i, j: (0, jnp.minimum(j, n_main - 1))),
            pl.BlockSpec((d, tn), lambda i, j: (0, jnp.maximum(j - n_main, 0))),
            pl.BlockSpec((d, 2 * LANES), lambda i, j: (0, 0)),
        ],
        out_specs=(
            pl.BlockSpec((tm, tn), lambda i, j: (i, j)),
            pl.BlockSpec((tm, LANES), lambda i, j: (i, 0)),
            pl.BlockSpec((tm, LANES), lambda i, j: (i, 0)),
        ),
        scratch_shapes=[pltpu.VMEM((tm, d), BF16)],
        compiler_params=_params(("arbitrary", "arbitrary")),
        name="proj",
    )(x2, g, sc, sh, w_main, w_gates, w_small)


CONV_HALO = 32
CONV_ROWS = 64
NORM_ROWS = 256


def _conv_kernel(a_ref, gt_ref, w_ref, bdw_ref, gln_ref, bln_ref, *rest):
    n_cast = (len(rest) - 5) // 2
    o_ref = rest[n_cast]
    u_ref, cv_ref, wb_ref, sh_ref = rest[2 * n_cast + 1:]
    _cast_slabs(rest[:n_cast] + rest[n_cast + 1:2 * n_cast + 1])

    s = pl.program_id(1)
    ts = a_ref.shape[0]
    dc = a_ref.shape[1]

    @pl.when(s == 0)
    def _():
        u_ref[0:CONV_HALO, :] = jnp.zeros((CONV_HALO, dc), F32)

    @pl.when(s > 0)
    def _():
        u_ref[0:CONV_HALO, :] = u_ref[ts:ts + CONV_HALO, :]

    def glu_body(r, carry):
        r0 = pl.multiple_of(r * CONV_ROWS, CONV_ROWS)
        a = a_ref[pl.ds(r0, CONV_ROWS), :].astype(F32)
        g = gt_ref[pl.ds(r0, CONV_ROWS), :].astype(F32)
        u_ref[pl.ds(CONV_HALO + r0, CONV_ROWS), :] = a * _sigmoid(g)
        return carry

    lax.fori_loop(0, ts // CONV_ROWS, glu_body, 0)

    for jtap in range(CONV_WIDTH):
        wb_ref[jtap * SUBLANES:(jtap + 1) * SUBLANES, :] = jnp.broadcast_to(
            w_ref[jtap:jtap + 1, :], (SUBLANES, dc))

    base = CONV_HALO - (CONV_WIDTH - 1)

    taps = {}
    for jtap in range(CONV_WIDTH):
        a, b = divmod(base + jtap, SUBLANES)
        taps.setdefault(b, []).append((a, jtap))

    def conv_body(r, carry):
        r0 = pl.multiple_of(r * CONV_ROWS, CONV_ROWS)
        for lc in range(dc // LANES):
            cols = slice(lc * LANES, (lc + 1) * LANES)
            win = u_ref[pl.ds(r0, CONV_ROWS + CONV_HALO), cols]
            for b, group in taps.items():
                span = CONV_ROWS + SUBLANES * max(a for a, _ in group)
                sh_ref[lc, b, 0:span, :] = win[b:b + span, :]
            acc = jnp.zeros((CONV_ROWS, LANES), F32)
            for b, group in taps.items():
                for a, jtap in group:
                    w_tap = wb_ref[jtap * SUBLANES:(jtap + 1) * SUBLANES, cols]
                    w_rows = jnp.concatenate([w_tap] * (CONV_ROWS // SUBLANES), axis=0)
                    acc = acc + sh_ref[lc, b, SUBLANES * a:SUBLANES * a + CONV_ROWS, :] * w_rows
            cv_ref[pl.ds(r0, CONV_ROWS), cols] = acc + bdw_ref[:, cols]
        return carry

    lax.fori_loop(0, ts // CONV_ROWS, conv_body, 0)

    def norm_body(r, carry):
        r0 = pl.multiple_of(r * NORM_ROWS, NORM_ROWS)
        xr = cv_ref[pl.ds(r0, NORM_ROWS), :]
        mu = jnp.mean(xr, axis=-1, keepdims=True)
        xc = xr - mu
        var = jnp.mean(xc * xc, axis=-1, keepdims=True)
        y = (xc * lax.rsqrt(var + EPS)) * gln_ref[...] + bln_ref[...]
        o_ref[pl.ds(r0, NORM_ROWS), :] = (y * _sigmoid(y)).astype(BF16)
        return carry

    lax.fori_loop(0, ts // NORM_ROWS, norm_body, 0)


def _conv(p, w_dw, b_dw, g_ln, b_ln, later_weights, batch, seq, ts=512):
    m = p.shape[0]
    dc = w_dw.shape[1]
    per_b = seq // ts
    cast_in, cast_out, cast_shapes = _cast_specs(
        later_weights, batch * per_b, lambda b, s: (b * per_b + s, 0))
    return pl.pallas_call(
        _conv_kernel,
        out_shape=[jax.ShapeDtypeStruct((m, dc), BF16)] + cast_shapes,
        grid=(batch, per_b),
        in_specs=[
            pl.BlockSpec((ts, dc), lambda b, s: (b * per_b + s, 0)),
            pl.BlockSpec((ts, dc), lambda b, s: (b * per_b + s, 1)),
            pl.BlockSpec((CONV_WIDTH, dc), lambda b, s: (0, 0)),
            pl.BlockSpec((1, dc), lambda b, s: (0, 0)),
            pl.BlockSpec((1, dc), lambda b, s: (0, 0)),
            pl.BlockSpec((1, dc), lambda b, s: (0, 0)),
        ] + cast_in,
        out_specs=[pl.BlockSpec((ts, dc), lambda b, s: (b * per_b + s, 0))] + cast_out,
        scratch_shapes=[
            pltpu.VMEM((ts + CONV_HALO, dc), F32),
            pltpu.VMEM((ts, dc), F32),
            pltpu.VMEM((CONV_WIDTH * SUBLANES, dc), F32),
            pltpu.VMEM((dc // LANES, SUBLANES, CONV_ROWS + CONV_HALO, LANES), F32),
        ],
        compiler_params=_params(("arbitrary", "arbitrary")),
        name="conv",
    )(p, p, w_dw, b_dw, g_ln, b_ln, *later_weights)


Q_TILE = 256
Q_SUB = 128
Q_SUBS = Q_TILE // Q_SUB
KEY_TILE = 256
KEY_SUB = 256
SEL_ROWS = 64
LANE_CHUNKS = KEY_TILE // LANES
INDEX_BITS = 13


def _fold_lanes(acc, t, op):
    for ch in range(t.shape[1] // LANES):
        acc = op(acc, t[:, ch * LANES:(ch + 1) * LANES])
    return acc


def _attn_kernel(qi_ref, q_ref, k_ref, v_ref, kk_ref, wi_ref, o_ref,
                 qim_ref, qm_ref, wrep_ref, key_ref, bias_ref, m_ref, acc_ref, thr_ref,
                 cut_ref, cand_ref, idx_ref, *, topk):
    i = pl.program_id(1)
    p0 = i * Q_TILE
    n_kt = (p0 + Q_TILE + KEY_TILE - 1) // KEY_TILE
    idx_scale = (IDX_DIM ** -0.5) * (IDX_HEADS ** -0.5)
    attn_scale = HEAD_DIM ** -0.5

    lane = lax.broadcasted_iota(I32, (Q_TILE, LANES), 1)
    lo = lane < HEAD_DIM
    lo_sub = lax.broadcasted_iota(I32, (Q_SUB, LANES), 1) < HEAD_DIM
    row = lax.broadcasted_iota(I32, (Q_TILE, KEY_TILE), 0)
    col = lax.broadcasted_iota(I32, (Q_TILE, KEY_TILE), 1)
    q_chunk = (p0 + row) >> CHUNK_SHIFT

    wi = wi_ref[...]
    for hp in range(N_HEADS // 2):
        cols = slice(hp * LANES, (hp + 1) * LANES)
        q_p = q_ref[:, cols].astype(F32) * (attn_scale * LOG2_E)
        qm_ref[2 * hp] = jnp.where(lo, q_p, 0.0).astype(BF16)
        qm_ref[2 * hp + 1] = jnp.where(lo, 0.0, q_p).astype(BF16)
        for rs in range(Q_SUBS):
            qi_p = qi_ref[rs * Q_SUB:(rs + 1) * Q_SUB, cols]
            zero_i = jnp.zeros_like(qi_p)
            qim_ref[rs, (2 * hp) * Q_SUB:(2 * hp + 1) * Q_SUB, :] = jnp.where(lo_sub, qi_p, zero_i)
            qim_ref[rs, (2 * hp + 1) * Q_SUB:(2 * hp + 2) * Q_SUB, :] = jnp.where(lo_sub, zero_i, qi_p)
    for h in range(IDX_HEADS):
        wrep_ref[h] = jnp.broadcast_to(wi[:, h:h + 1], (Q_TILE, LANES))

    sub_row = lax.broadcasted_iota(I32, (Q_SUB, KEY_SUB), 0)
    sub_col = lax.broadcasted_iota(I32, (Q_SUB, KEY_SUB), 1)

    def score_body(kt, carry):
        k0 = pl.multiple_of(kt * KEY_TILE, KEY_TILE)
        for ks in range(KEY_TILE // KEY_SUB):
            kk_s = kk_ref[pl.ds(k0 + ks * KEY_SUB, KEY_SUB), :]
            for rs in range(Q_SUBS):
                rows = slice(rs * Q_SUB, (rs + 1) * Q_SUB)
                rel = _dot_nt(qim_ref[rs], kk_s)
                acc = [jnp.zeros((Q_SUB, LANES), F32) for _ in range(KEY_SUB // LANES)]
                for h in range(IDX_HEADS):
                    w_h = wrep_ref[h, rows, :]
                    for ch in range(KEY_SUB // LANES):
                        r = rel[h * Q_SUB:(h + 1) * Q_SUB, ch * LANES:(ch + 1) * LANES]
                        acc[ch] = acc[ch] + jnp.maximum(r, 0.0) * w_h
                score = jnp.concatenate(acc, axis=-1) * idx_scale
                adm = ((k0 + ks * KEY_SUB + sub_col) >> CHUNK_SHIFT) <= (
                    (p0 + rs * Q_SUB + sub_row) >> CHUNK_SHIFT)
                key_ref[kt, rows, ks * KEY_SUB:(ks + 1) * KEY_SUB] = jnp.where(adm, score, -jnp.inf)
        return carry

    lax.fori_loop(0, n_kt, score_body, 0)

    kf = float(topk)

    def key_to_float(key_s):
        bits = key_s ^ ((key_s >> 31) & 0x7FFFFFFF)
        return jnp.where(key_s <= NEG_INF_KEY, -jnp.inf, pltpu.bitcast(bits, F32))

    lane_s = lax.broadcasted_iota(I32, (SEL_ROWS, LANES), 1)

    row_groups = [slice(rg * SEL_ROWS, (rg + 1) * SEL_ROWS) for rg in range(Q_TILE // SEL_ROWS)]

    def count(indicator):
        def body(kt, cs):
            out = []
            for rows, c in zip(row_groups, cs):
                for ch in range(LANE_CHUNKS):
                    kc = key_ref[kt, rows, ch * LANES:(ch + 1) * LANES]
                    c = c + indicator(kc, kt * KEY_TILE + ch * LANES + lane_s, rows)
                out.append(c)
            return tuple(out)

        zero = jnp.zeros((SEL_ROWS, LANES), F32)
        cs = lax.fori_loop(0, n_kt, body, tuple(zero for _ in row_groups))
        return jnp.sum(jnp.concatenate(cs, axis=0), axis=-1, keepdims=True)

    def bisect_body(it, prefix):
        bit = lax.shift_left(jnp.int32(1), jnp.int32(31) - it)
        cand_u = prefix | bit
        cand_ref[...] = key_to_float(cand_u ^ INT_MIN)
        cnt = count(lambda kc, _, rows: jnp.where(kc >= cand_ref[rows, :], 1.0, 0.0))
        return jnp.where(cnt >= kf, cand_u, prefix)

    prefix = lax.fori_loop(0, 32, bisect_body, jnp.zeros((Q_TILE, LANES), I32))
    thr = key_to_float(prefix ^ INT_MIN)
    thr_ref[...] = thr

    n_gt = count(lambda kc, _, rows: jnp.where(kc > thr_ref[rows, :], 1.0, 0.0))
    n_ge = count(lambda kc, _, rows: jnp.where(kc >= thr_ref[rows, :], 1.0, 0.0))
    want = kf - n_gt
    cut_ref[...] = jnp.full((Q_TILE, LANES), 2 ** INDEX_BITS, I32)
    tied = jnp.where(thr[:, :1] == -jnp.inf, 0.0, jnp.where(n_ge > kf, 1.0, 0.0))
    surplus = jnp.max(tied)

    @pl.when(surplus > 0.0)
    def _():
        def cut_body(it, cut):
            bit = lax.shift_left(jnp.int32(1), jnp.int32(INDEX_BITS - 1) - it)
            cand = cut | bit
            idx_ref[...] = cand
            cnt = count(lambda kc, idx, rows: jnp.where(
                kc == thr_ref[rows, :], jnp.where(idx < idx_ref[rows, :], 1.0, 0.0), 0.0))
            return jnp.where(cnt <= want, cand, cut)

        cut_ref[...] = lax.fori_loop(0, INDEX_BITS, cut_body, jnp.zeros((Q_TILE, LANES), I32))

    cut = cut_ref[...]

    def bias_body(kt, carry):
        k0 = pl.multiple_of(kt * KEY_TILE, KEY_TILE)
        kt_keys = key_ref[kt]
        adm = ((k0 + col) >> CHUNK_SHIFT) <= q_chunk
        pieces = []
        for ch in range(LANE_CHUNKS):
            kc = kt_keys[:, ch * LANES:(ch + 1) * LANES]
            idx = k0 + ch * LANES + lane
            tie = jnp.where(kc == thr, jnp.where(idx < cut, 0.0, -jnp.inf), -jnp.inf)
            pieces.append(jnp.where(kc > thr, 0.0, tie))
        b = jnp.concatenate(pieces, axis=-1)
        bias_ref[kt] = jnp.where(adm, b, -jnp.inf)
        return carry

    lax.fori_loop(0, n_kt, bias_body, 0)

    for h in range(N_HEADS):
        m_ref[h] = jnp.full((Q_TILE, LANES), NEG_BIG, F32)
        acc_ref[h] = jnp.zeros((Q_TILE, LANES), F32)

    lo_kv = lax.broadcasted_iota(I32, (KEY_TILE, LANES), 1) < HEAD_DIM

    def attn_body(kt, carry):
        k0 = pl.multiple_of(kt * KEY_TILE, KEY_TILE)
        for hp in range(N_HEADS // 2):
            cols = slice(hp * LANES, (hp + 1) * LANES)
            k_t = k_ref[pl.ds(k0, KEY_TILE), cols]
            v_t = v_ref[pl.ds(k0, KEY_TILE), cols]
            one = jnp.ones_like(v_t)
            v_ext = (jnp.where(lo_kv, v_t, one), jnp.where(lo_kv, one, v_t))
            for par in range(2):
                h = 2 * hp + par
                s = _dot_nt(qm_ref[h], k_t) + bias_ref[kt]
                t_max = jnp.max(_fold_lanes(jnp.full((Q_TILE, LANES), -jnp.inf, F32), s, jnp.maximum),
                                axis=-1, keepdims=True)
                m_old = m_ref[h]
                m_new = jnp.maximum(m_old, t_max)
                alpha = jnp.exp2(m_old - m_new)
                e = jnp.concatenate(
                    [jnp.exp2(s[:, ch * LANES:(ch + 1) * LANES] - m_new) for ch in range(LANE_CHUNKS)],
                    axis=-1)
                acc_ref[h] = alpha * acc_ref[h] + _dot(e.astype(BF16), v_ext[par])
                m_ref[h] = m_new
        return carry

    lax.fori_loop(0, n_kt, attn_body, 0)

    for hp in range(N_HEADS // 2):
        acc_e = acc_ref[2 * hp]
        acc_o = acc_ref[2 * hp + 1]
        num = jnp.where(lo, acc_e, acc_o)
        den = pltpu.roll(jnp.where(lo, acc_o, acc_e), HEAD_DIM, 1)
        o_ref[:, hp * LANES:(hp + 1) * LANES] = (num / den).astype(BF16)


def _attn(p, kk, wi, batch, seq, topk):
    m = p.shape[0]
    nq = seq // Q_TILE
    n_kt = seq // KEY_TILE
    once = pl.Buffered(1)
    return pl.pallas_call(
        functools.partial(_attn_kernel, topk=topk),
        out_shape=jax.ShapeDtypeStruct((m, D_ATTN), BF16),
        grid=(batch, nq),
        in_specs=[
            pl.BlockSpec((Q_TILE, D_ATTN), lambda b, i: (b * nq + i, 5)),
            pl.BlockSpec((Q_TILE, D_ATTN), lambda b, i: (b * nq + i, 2)),
            pl.BlockSpec((seq, D_ATTN), lambda b, i: (b, 3), pipeline_mode=once),
            pl.BlockSpec((seq, D_ATTN), lambda b, i: (b, 4), pipeline_mode=once),
            pl.BlockSpec((seq, LANES), lambda b, i: (b, 0), pipeline_mode=once),
            pl.BlockSpec((Q_TILE, LANES), lambda b, i: (b * nq + i, 0)),
        ],
        out_specs=pl.BlockSpec((Q_TILE, D_ATTN), lambda b, i: (b * nq + i, 0)),
        scratch_shapes=[
            pltpu.VMEM((Q_SUBS, IDX_HEADS * Q_SUB, LANES), BF16),
            pltpu.VMEM((N_HEADS, Q_TILE, LANES), BF16),
            pltpu.VMEM((IDX_HEADS, Q_TILE, LANES), F32),
            pltpu.VMEM((n_kt, Q_TILE, KEY_TILE), F32),
            pltpu.VMEM((n_kt, Q_TILE, KEY_TILE), F32),
            pltpu.VMEM((N_HEADS, Q_TILE, LANES), F32),
            pltpu.VMEM((N_HEADS, Q_TILE, LANES), F32),
            pltpu.VMEM((Q_TILE, LANES), F32),
            pltpu.VMEM((Q_TILE, LANES), I32),
            pltpu.VMEM((Q_TILE, LANES), F32),
            pltpu.VMEM((Q_TILE, LANES), I32),
        ],
        compiler_params=_params(("arbitrary", "arbitrary")),
        name="attn",
    )(p, p, p, p, kk, wi)


MERGE_ROWS = 256


def _rms_rows(x, g):
    ms = jnp.mean(x * x, axis=-1, keepdims=True)
    return (x * lax.rsqrt(ms + EPS)) * g


def _merge_kernel(uc_ref, oa_ref, gc_ref, ga_ref, x_ref, gt_ref, gp_ref,
                  wc_ref, wa_ref, wo_ref, *rest):
    n_cast = (len(rest) - 1) // 2
    o_ref = rest[n_cast]
    _cast_slabs(rest[:n_cast] + rest[n_cast + 1:])

    tm = x_ref.shape[0]
    gt = gt_ref[0]
    gp = gp_ref[...]

    def body(r, carry):
        r0 = pl.multiple_of(r * MERGE_ROWS, MERGE_ROWS)
        rows = pl.ds(r0, MERGE_ROWS)
        yc = _dot(uc_ref[rows, :], wc_ref[...])
        ya = _dot(oa_ref[rows, :], wa_ref[...])
        gc = _sigmoid(gc_ref[rows, :].astype(F32))
        ga = _sigmoid(ga_ref[rows, :].astype(F32))
        merged = (gc * yc + ga * ya).astype(BF16)
        mixed = _dot(merged, wo_ref[...])
        o_ref[rows, :] = x_ref[rows, :] + gt * _rms_rows(mixed, gp)
        return carry

    lax.fori_loop(0, tm // MERGE_ROWS, body, 0)


def _cast_specs(weights, steps, index_map):
    in_specs, out_specs, out_shapes = [], [], []
    for w in weights:
        rows, cols = w.shape
        slab = rows // steps
        assert slab * steps == rows and slab % (2 * SUBLANES) == 0, (w.shape, steps)
        in_specs.append(pl.BlockSpec((slab, cols), index_map))
        out_specs.append(pl.BlockSpec((slab, cols), index_map))
        out_shapes.append(jax.ShapeDtypeStruct((rows, cols), BF16))
    return in_specs, out_specs, out_shapes


def _cast_slabs(refs):
    half = len(refs) // 2
    for src, dst in zip(refs[:half], refs[half:]):
        dst[...] = src[...].astype(BF16)


def _merge(uc, oa, p, x2, gt1, g_post, wc, wa, wo, next_weights, seq, tm=256):
    m, d = x2.shape
    dc = uc.shape[1]
    per_b = seq // tm
    gate_blk = p.shape[1] // d - 2
    once = pl.Buffered(1)
    cast_in, cast_out, cast_shapes = _cast_specs(next_weights, m // tm, lambda i: (i, 0))
    return pl.pallas_call(
        _merge_kernel,
        out_shape=[jax.ShapeDtypeStruct((m, d), F32)] + cast_shapes,
        grid=(m // tm,),
        in_specs=[
            pl.BlockSpec((tm, dc), lambda i: (i, 0)),
            pl.BlockSpec((tm, dc), lambda i: (i, 0)),
            pl.BlockSpec((tm, d), lambda i: (i, gate_blk)),
            pl.BlockSpec((tm, d), lambda i: (i, gate_blk + 1)),
            pl.BlockSpec((tm, d), lambda i: (i, 0)),
            pl.BlockSpec((1, 1, d), lambda i: (i // per_b, 0, 0)),
            pl.BlockSpec((1, d), lambda i: (0, 0)),
            pl.BlockSpec((dc, d), lambda i: (0, 0), pipeline_mode=once),
            pl.BlockSpec((dc, d), lambda i: (0, 0), pipeline_mode=once),
            pl.BlockSpec((d, d), lambda i: (0, 0), pipeline_mode=once),
        ] + cast_in,
        out_specs=[pl.BlockSpec((tm, d), lambda i: (i, 0))] + cast_out,
        compiler_params=_params(("arbitrary",)),
        name="merge",
    )(uc, oa, p, p, x2, gt1, g_post, wc, wa, wo, *next_weights)


FFN_ROWS = 512


def _ffn_kernel(x_ref, g_ref, sc_ref, sh_ref, gt_ref, gp_ref, wg_ref, wu_ref, wd_ref,
                o_ref, h_ref, acc_ref):
    j = pl.program_id(1)
    tm = x_ref.shape[0]
    n_rows = tm // FFN_ROWS

    @pl.when(j == 0)
    def _():
        g = g_ref[...]
        sc = sc_ref[0]
        sh = sh_ref[0]

        def body(r, carry):
            rows = pl.ds(pl.multiple_of(r * ROW_CHUNK, ROW_CHUNK), ROW_CHUNK)
            h_ref[rows, :] = _adaln_rows(x_ref[rows, :], g, sc, sh).astype(BF16)
            acc_ref[rows, :] = jnp.zeros((ROW_CHUNK, acc_ref.shape[1]), F32)
            return carry

        lax.fori_loop(0, tm // ROW_CHUNK, body, 0)

    for r in range(n_rows):
        rows = slice(r * FFN_ROWS, (r + 1) * FFN_ROWS)
        h = h_ref[rows, :]
        a = _dot(h, wg_ref[...])
        b = _dot(h, wu_ref[...])
        act = ((a * _sigmoid(a)) * b).astype(BF16)
        acc_ref[rows, :] = acc_ref[rows, :] + _dot(act, wd_ref[...])

    @pl.when(j == pl.num_programs(1) - 1)
    def _():
        gt = gt_ref[0]
        gp = gp_ref[...]

        def body(r, carry):
            rows = pl.ds(pl.multiple_of(r * ROW_CHUNK, ROW_CHUNK), ROW_CHUNK)
            o_ref[rows, :] = x_ref[rows, :] + gt * _rms_rows(acc_ref[rows, :], gp)
            return carry

        lax.fori_loop(0, tm // ROW_CHUNK, body, 0)


def _ffn(x2, g_pre, sc2, sh2, gt2, g_post, wg, wu, wd, seq, tm=512, tf=512):
    m, d = x2.shape
    dff = wg.shape[1]
    per_b = seq // tm
    return pl.pallas_call(
        _ffn_kernel,
        out_shape=jax.ShapeDtypeStruct((m, d), F32),
        grid=(m // tm, dff // tf),
        in_specs=[
            pl.BlockSpec((tm, d), lambda i, j: (i, 0)),
            pl.BlockSpec((1, d), lambda i, j: (0, 0)),
            pl.BlockSpec((1, 1, d), lambda i, j: (i // per_b, 0, 0)),
            pl.BlockSpec((1, 1, d), lambda i, j: (i // per_b, 0, 0)),
            pl.BlockSpec((1, 1, d), lambda i, j: (i // per_b, 0, 0)),
            pl.BlockSpec((1, d), lambda i, j: (0, 0)),
            pl.BlockSpec((d, tf), lambda i, j: (0, j)),
            pl.BlockSpec((d, tf), lambda i, j: (0, j)),
            pl.BlockSpec((tf, d), lambda i, j: (j, 0)),
        ],
        out_specs=pl.BlockSpec((tm, d), lambda i, j: (i, 0)),
        scratch_shapes=[pltpu.VMEM((tm, d), BF16), pltpu.VMEM((tm, d), F32)],
        compiler_params=_params(("arbitrary", "arbitrary")),
        name="ffn",
    )(x2, g_pre, sc2, sh2, gt2, g_post, wg, wu, wd)


def kernel(x, c, w_ada, b_ada, g_pre_mix, w_in, w_dw, b_dw, g_conv_ln, b_conv_ln, w_conv_out, w_attn_out, w_o, g_post_mix, g_pre_ffn, w_gate, w_up, w_down, g_post_ffn):
    batch, seq, d = x.shape
    depth = w_ada.shape[0]
    topk = min(TOPK_MAX, seq // 4)

    off_glu = 2 * D_CONV
    off_ki = off_glu + 3 * D_ATTN + IDX_HEADS * IDX_DIM
    off_wi = off_ki + IDX_DIM
    off_gate = off_wi + IDX_HEADS

    c8 = jnp.zeros((SUBLANES, d), F32).at[:batch].set(c)
    x2 = x.reshape(batch * seq, d)
    for l in range(depth):
        mod, w_bf = _ada(c8, w_ada[l], b_ada[l][None, :], (w_in[l],))
        mod = mod[:batch]
        sh1, sc1, gt1, sh2, sc2, gt2 = [t[:, None, :] for t in jnp.split(mod, 6, axis=-1)]

        w_gates = w_bf[:, off_gate:]
        w_ki = w_bf[:, off_ki:off_wi]
        w_wi = jnp.pad(w_bf[:, off_wi:off_gate], ((0, 0), (0, LANES - IDX_HEADS)))
        w_small = jnp.concatenate([w_ki, w_ki, w_wi], axis=1)

        p, kk, wi = _proj(x2, g_pre_mix[l][None, :], sc1, sh1, w_bf, w_gates, w_small, seq,
                          n_main_cols=off_ki)
        uc, wc_bf, wa_bf, wo_bf = _conv(
            p, w_dw[l], b_dw[l][None, :], g_conv_ln[l][None, :], b_conv_ln[l][None, :],
            (w_conv_out[l], w_attn_out[l], w_o[l]), batch, seq)
        oa = _attn(p, kk, wi, batch, seq, topk)
        x2, wg_bf, wu_bf, wd_bf = _merge(
            uc, oa, p, x2, gt1, g_post_mix[l][None, :], wc_bf, wa_bf, wo_bf,
            (w_gate[l], w_up[l], w_down[l]), seq)
        x2 = _ffn(x2, g_pre_ffn[l][None, :], sc2, sh2, gt2, g_post_ffn[l][None, :],
                  wg_bf, wu_bf, wd_bf, seq)
    return x2.reshape(batch, seq, d)
```

```python
import functools

import jax
import jax.numpy as jnp
from jax import lax
from jax.experimental import pallas as pl
from jax.experimental.pallas import tpu as pltpu

F32 = jnp.float32
BF16 = jnp.bfloat16
I32 = jnp.int32

CHUNK = 64
CHUNK_SHIFT = 6
D_CONV = 1024
CONV_WIDTH = 31
N_HEADS = 16
HEAD_DIM = 64
D_ATTN = N_HEADS * HEAD_DIM
IDX_HEADS = 16
IDX_DIM = 64
TOPK_MAX = 256
EPS = 1e-6

LANES = 128
SUBLANES = 8
VMEM_LIMIT_BYTES = 56 * 1024 * 1024

LOG2_E = 1.4426950408889634
NEG_BIG = -1e30
INT_MIN = -(2 ** 31)
NEG_INF_KEY = (0xFF800000 ^ 0x7FFFFFFF) - 2 ** 32


def _sigmoid(x):
    return 1.0 / (1.0 + jnp.exp(-x))


def _dot(a, b):
    return jnp.dot(a, b, preferred_element_type=F32)


def _dot_nt(a, b):
    return lax.dot_general(a, b, (((1,), (1,)), ((), ())), preferred_element_type=F32)


def _params(sem):
    return pltpu.CompilerParams(dimension_semantics=sem, vmem_limit_bytes=VMEM_LIMIT_BYTES)


def _ada_kernel(c_ref, w_ref, b_ref, o_ref):
    c = c_ref[...]
    ca = (c * _sigmoid(c)).astype(BF16)
    o_ref[...] = _dot(ca, w_ref[...].astype(BF16)) + b_ref[...]


def _ada(c8, w, b, tn=1024):
    rows, d = c8.shape
    n = w.shape[1]
    return pl.pallas_call(
        _ada_kernel,
        out_shape=jax.ShapeDtypeStruct((rows, n), F32),
        grid=(n // tn,),
        in_specs=[
            pl.BlockSpec((rows, d), lambda j: (0, 0)),
            pl.BlockSpec((d, tn), lambda j: (0, j)),
            pl.BlockSpec((1, tn), lambda j: (0, j)),
        ],
        out_specs=pl.BlockSpec((rows, tn), lambda j: (0, j)),
        compiler_params=_params(("arbitrary",)),
        name="ada",
    )(c8, w, b)


ROW_CHUNK = 128


def _adaln_rows(x, g, sc, sh):
    ms = jnp.mean(x * x, axis=-1, keepdims=True)
    y = (x * lax.rsqrt(ms + EPS)) * g
    return y * (1.0 + sc) + sh


def _proj_kernel(x_ref, g_ref, sc_ref, sh_ref, w_ref, wg_ref, ws_ref, o_ref, kk_ref, wi_ref, h_ref,
                 *, n_main):
    j = pl.program_id(1)
    tm = x_ref.shape[0]

    @pl.when(j == 0)
    def _():
        g = g_ref[...]
        sc = sc_ref[0]
        sh = sh_ref[0]

        def body(r, carry):
            r0 = pl.multiple_of(r * ROW_CHUNK, ROW_CHUNK)
            h = _adaln_rows(x_ref[pl.ds(r0, ROW_CHUNK), :], g, sc, sh).astype(BF16)
            h_ref[pl.ds(r0, ROW_CHUNK), :] = h
            small = _dot(h, ws_ref[...])
            kk_ref[pl.ds(r0, ROW_CHUNK), :] = small[:, :LANES].astype(BF16)
            wi_ref[pl.ds(r0, ROW_CHUNK), :] = small[:, LANES:]
            return carry

        lax.fori_loop(0, tm // ROW_CHUNK, body, 0)

    @pl.when(j < n_main)
    def _():
        o_ref[...] = _dot(h_ref[...], w_ref[...]).astype(BF16)

    @pl.when(j >= n_main)
    def _():
        o_ref[...] = _dot(h_ref[...], wg_ref[...]).astype(BF16)


def _proj(x2, g, sc, sh, w_main, w_gates, w_small, seq, n_main_cols, tm=1024, tn=1024):
    m, d = x2.shape
    n_main = n_main_cols // tn
    n = n_main_cols + w_gates.shape[1]
    per_b = seq // tm
    return pl.pallas_call(
        functools.partial(_proj_kernel, n_main=n_main),
        out_shape=(
            jax.ShapeDtypeStruct((m, n), BF16),
            jax.ShapeDtypeStruct((m, LANES), BF16),
            jax.ShapeDtypeStruct((m, LANES), F32),
        ),
        grid=(m // tm, n // tn),
        in_specs=[
            pl.BlockSpec((tm, d), lambda i, j: (i, 0)),
            pl.BlockSpec((1, d), lambda i, j: (0, 0)),
            pl.BlockSpec((1, 1, d), lambda i, j: (i // per_b, 0, 0)),
            pl.BlockSpec((1, 1, d), lambda i, j: (i // per_b, 0, 0)),
            pl.BlockSpec((d, tn), lambda i, j: (0, jnp.minimum(j, n_main - 1))),
            pl.BlockSpec((d, tn), lambda i, j: (0, jnp.maximum(j - n_main, 0))),
            pl.BlockSpec((d, 2 * LANES), lambda i, j: (0, 0)),
        ],
        out_specs=(
            pl.BlockSpec((tm, tn), lambda i, j: (i, j)),
            pl.BlockSpec((tm, LANES), lambda i, j: (i, 0)),
            pl.BlockSpec((tm, LANES), lambda i, j: (i, 0)),
        ),
        scratch_shapes=[pltpu.VMEM((tm, d), BF16)],
        compiler_params=_params(("arbitrary", "arbitrary")),
        name="proj",
    )(x2, g, sc, sh, w_main, w_gates, w_small)


CONV_HALO = 32
CONV_ROWS = 64
NORM_ROWS = 256


def _conv_kernel(a_ref, gt_ref, w_ref, bdw_ref, gln_ref, bln_ref, *rest):
    n_cast = (len(rest) - 5) // 2
    o_ref = rest[n_cast]
    u_ref, cv_ref, wb_ref, sh_ref = rest[2 * n_cast + 1:]
    _cast_slabs(rest[:n_cast] + rest[n_cast + 1:2 * n_cast + 1])

    s = pl.program_id(1)
    ts = a_ref.shape[0]
    dc = a_ref.shape[1]

    @pl.when(s == 0)
    def _():
        u_ref[0:CONV_HALO, :] = jnp.zeros((CONV_HALO, dc), F32)

    @pl.when(s > 0)
    def _():
        u_ref[0:CONV_HALO, :] = u_ref[ts:ts + CONV_HALO, :]

    def glu_body(r, carry):
        r0 = pl.multiple_of(r * CONV_ROWS, CONV_ROWS)
        a = a_ref[pl.ds(r0, CONV_ROWS), :].astype(F32)
        g = gt_ref[pl.ds(r0, CONV_ROWS), :].astype(F32)
        u_ref[pl.ds(CONV_HALO + r0, CONV_ROWS), :] = a * _sigmoid(g)
        return carry

    lax.fori_loop(0, ts // CONV_ROWS, glu_body, 0)

    for jtap in range(CONV_WIDTH):
        wb_ref[jtap * SUBLANES:(jtap + 1) * SUBLANES, :] = jnp.broadcast_to(
            w_ref[jtap:jtap + 1, :], (SUBLANES, dc))

    base = CONV_HALO - (CONV_WIDTH - 1)

    taps = {}
    for jtap in range(CONV_WIDTH):
        a, b = divmod(base + jtap, SUBLANES)
        taps.setdefault(b, []).append((a, jtap))

    def conv_body(r, carry):
        r0 = pl.multiple_of(r * CONV_ROWS, CONV_ROWS)
        for lc in range(dc // LANES):
            cols = slice(lc * LANES, (lc + 1) * LANES)
            win = u_ref[pl.ds(r0, CONV_ROWS + CONV_HALO), cols]
            for b, group in taps.items():
                span = CONV_ROWS + SUBLANES * max(a for a, _ in group)
                sh_ref[lc, b, 0:span, :] = win[b:b + span, :]
            acc = jnp.zeros((CONV_ROWS, LANES), F32)
            for b, group in taps.items():
                for a, jtap in group:
                    w_tap = wb_ref[jtap * SUBLANES:(jtap + 1) * SUBLANES, cols]
                    w_rows = jnp.concatenate([w_tap] * (CONV_ROWS // SUBLANES), axis=0)
                    acc = acc + sh_ref[lc, b, SUBLANES * a:SUBLANES * a + CONV_ROWS, :] * w_rows
            cv_ref[pl.ds(r0, CONV_ROWS), cols] = acc + bdw_ref[:, cols]
        return carry

    lax.fori_loop(0, ts // CONV_ROWS, conv_body, 0)

    def norm_body(r, carry):
        r0 = pl.multiple_of(r * NORM_ROWS, NORM_ROWS)
        xr = cv_ref[pl.ds(r0, NORM_ROWS), :]
        mu = jnp.mean(xr, axis=-1, keepdims=True)
        xc = xr - mu
        var = jnp.mean(xc * xc, axis=-1, keepdims=True)
        y = (xc * lax.rsqrt(var + EPS)) * gln_ref[...] + bln_ref[...]
        o_ref[pl.ds(r0, NORM_ROWS), :] = (y * _sigmoid(y)).astype(BF16)
        return carry

    lax.fori_loop(0, ts // NORM_ROWS, norm_body, 0)


def _conv(p, w_dw, b_dw, g_ln, b_ln, later_weights, batch, seq, ts=512):
    m = p.shape[0]
    dc = w_dw.shape[1]
    per_b = seq // ts
    cast_in, cast_out, cast_shapes = _cast_specs(
        later_weights, batch * per_b, lambda b, s: (b * per_b + s, 0))
    return pl.pallas_call(
        _conv_kernel,
        out_shape=[jax.ShapeDtypeStruct((m, dc), BF16)] + cast_shapes,
        grid=(batch, per_b),
        in_specs=[
            pl.BlockSpec((ts, dc), lambda b, s: (b * per_b + s, 0)),
            pl.BlockSpec((ts, dc), lambda b, s: (b * per_b + s, 1)),
            pl.BlockSpec((CONV_WIDTH, dc), lambda b, s: (0, 0)),
            pl.BlockSpec((1, dc), lambda b, s: (0, 0)),
            pl.BlockSpec((1, dc), lambda b, s: (0, 0)),
            pl.BlockSpec((1, dc), lambda b, s: (0, 0)),
        ] + cast_in,
        out_specs=[pl.BlockSpec((ts, dc), lambda b, s: (b * per_b + s, 0))] + cast_out,
        scratch_shapes=[
            pltpu.VMEM((ts + CONV_HALO, dc), F32),
            pltpu.VMEM((ts, dc), F32),
            pltpu.VMEM((CONV_WIDTH * SUBLANES, dc), F32),
            pltpu.VMEM((dc // LANES, SUBLANES, CONV_ROWS + CONV_HALO, LANES), F32),
        ],
        compiler_params=_params(("arbitrary", "arbitrary")),
        name="conv",
    )(p, p, w_dw, b_dw, g_ln, b_ln, *later_weights)


Q_TILE = 256
Q_SUB = 128
Q_SUBS = Q_TILE // Q_SUB
KEY_TILE = 256
KEY_SUB = 256
SEL_ROWS = 64
LANE_CHUNKS = KEY_TILE // LANES
INDEX_BITS = 13


def _fold_lanes(acc, t, op):
    for ch in range(t.shape[1] // LANES):
        acc = op(acc, t[:, ch * LANES:(ch + 1) * LANES])
    return acc


def _attn_kernel(qi_ref, q_ref, k_ref, v_ref, kk_ref, wi_ref, o_ref,
                 qim_ref, qm_ref, wrep_ref, key_ref, bias_ref, m_ref, acc_ref, thr_ref,
                 cut_ref, cand_ref, idx_ref, *, topk):
    i = pl.program_id(1)
    p0 = i * Q_TILE
    n_kt = (p0 + Q_TILE + KEY_TILE - 1) // KEY_TILE
    idx_scale = (IDX_DIM ** -0.5) * (IDX_HEADS ** -0.5)
    attn_scale = HEAD_DIM ** -0.5

    lane = lax.broadcasted_iota(I32, (Q_TILE, LANES), 1)
    lo = lane < HEAD_DIM
    lo_sub = lax.broadcasted_iota(I32, (Q_SUB, LANES), 1) < HEAD_DIM
    row = lax.broadcasted_iota(I32, (Q_TILE, KEY_TILE), 0)
    col = lax.broadcasted_iota(I32, (Q_TILE, KEY_TILE), 1)
    q_chunk = (p0 + row) >> CHUNK_SHIFT

    wi = wi_ref[...]
    for hp in range(N_HEADS // 2):
        cols = slice(hp * LANES, (hp + 1) * LANES)
        q_p = q_ref[:, cols].astype(F32) * (attn_scale * LOG2_E)
        qm_ref[2 * hp] = jnp.where(lo, q_p, 0.0).astype(BF16)
        qm_ref[2 * hp + 1] = jnp.where(lo, 0.0, q_p).astype(BF16)
        for rs in range(Q_SUBS):
            qi_p = qi_ref[rs * Q_SUB:(rs + 1) * Q_SUB, cols]
            zero_i = jnp.zeros_like(qi_p)
            qim_ref[rs, (2 * hp) * Q_SUB:(2 * hp + 1) * Q_SUB, :] = jnp.where(lo_sub, qi_p, zero_i)
            qim_ref[rs, (2 * hp + 1) * Q_SUB:(2 * hp + 2) * Q_SUB, :] = jnp.where(lo_sub, zero_i, qi_p)
    for h in range(IDX_HEADS):
        wrep_ref[h] = jnp.broadcast_to(wi[:, h:h + 1], (Q_TILE, LANES))

    sub_row = lax.broadcasted_iota(I32, (Q_SUB, KEY_SUB), 0)
    sub_col = lax.broadcasted_iota(I32, (Q_SUB, KEY_SUB), 1)

    def score_body(kt, carry):
        k0 = pl.multiple_of(kt * KEY_TILE, KEY_TILE)
        for ks in range(KEY_TILE // KEY_SUB):
            kk_s = kk_ref[pl.ds(k0 + ks * KEY_SUB, KEY_SUB), :]
            for rs in range(Q_SUBS):
                rows = slice(rs * Q_SUB, (rs + 1) * Q_SUB)
                rel = _dot_nt(qim_ref[rs], kk_s)
                acc = [jnp.zeros((Q_SUB, LANES), F32) for _ in range(KEY_SUB // LANES)]
                for h in range(IDX_HEADS):
                    w_h = wrep_ref[h, rows, :]
                    for ch in range(KEY_SUB // LANES):
                        r = rel[h * Q_SUB:(h + 1) * Q_SUB, ch * LANES:(ch + 1) * LANES]
                        acc[ch] = acc[ch] + jnp.maximum(r, 0.0) * w_h
                score = jnp.concatenate(acc, axis=-1) * idx_scale
                adm = ((k0 + ks * KEY_SUB + sub_col) >> CHUNK_SHIFT) <= (
                    (p0 + rs * Q_SUB + sub_row) >> CHUNK_SHIFT)
                key_ref[kt, rows, ks * KEY_SUB:(ks + 1) * KEY_SUB] = jnp.where(adm, score, -jnp.inf)
        return carry

    lax.fori_loop(0, n_kt, score_body, 0)

    kf = float(topk)

    def key_to_float(key_s):
        bits = key_s ^ ((key_s >> 31) & 0x7FFFFFFF)
        return jnp.where(key_s <= NEG_INF_KEY, -jnp.inf, pltpu.bitcast(bits, F32))

    lane_s = lax.broadcasted_iota(I32, (SEL_ROWS, LANES), 1)

    row_groups = [slice(rg * SEL_ROWS, (rg + 1) * SEL_ROWS) for rg in range(Q_TILE // SEL_ROWS)]

    def count(indicator):
        def body(kt, cs):
            out = []
            for rows, c in zip(row_groups, cs):
                for ch in range(LANE_CHUNKS):
                    kc = key_ref[kt, rows, ch * LANES:(ch + 1) * LANES]
                    c = c + indicator(kc, kt * KEY_TILE + ch * LANES + lane_s, rows)
                out.append(c)
            return tuple(out)

        zero = jnp.zeros((SEL_ROWS, LANES), F32)
        cs = lax.fori_loop(0, n_kt, body, tuple(zero for _ in row_groups))
        return jnp.sum(jnp.concatenate(cs, axis=0), axis=-1, keepdims=True)

    def bisect_body(it, prefix):
        bit = lax.shift_left(jnp.int32(1), jnp.int32(31) - it)
        cand_u = prefix | bit
        cand_ref[...] = key_to_float(cand_u ^ INT_MIN)
        cnt = count(lambda kc, _, rows: jnp.where(kc >= cand_ref[rows, :], 1.0, 0.0))
        return jnp.where(cnt >= kf, cand_u, prefix)

    prefix = lax.fori_loop(0, 32, bisect_body, jnp.zeros((Q_TILE, LANES), I32))
    thr = key_to_float(prefix ^ INT_MIN)
    thr_ref[...] = thr

    n_gt = count(lambda kc, _, rows: jnp.where(kc > thr_ref[rows, :], 1.0, 0.0))
    n_ge = count(lambda kc, _, rows: jnp.where(kc >= thr_ref[rows, :], 1.0, 0.0))
    want = kf - n_gt
    cut_ref[...] = jnp.full((Q_TILE, LANES), 2 ** INDEX_BITS, I32)
    tied = jnp.where(thr[:, :1] == -jnp.inf, 0.0, jnp.where(n_ge > kf, 1.0, 0.0))
    surplus = jnp.max(tied)

    @pl.when(surplus > 0.0)
    def _():
        def cut_body(it, cut):
            bit = lax.shift_left(jnp.int32(1), jnp.int32(INDEX_BITS - 1) - it)
            cand = cut | bit
            idx_ref[...] = cand
            cnt = count(lambda kc, idx, rows: jnp.where(
                kc == thr_ref[rows, :], jnp.where(idx < idx_ref[rows, :], 1.0, 0.0), 0.0))
            return jnp.where(cnt <= want, cand, cut)

        cut_ref[...] = lax.fori_loop(0, INDEX_BITS, cut_body, jnp.zeros((Q_TILE, LANES), I32))

    cut = cut_ref[...]

    def bias_body(kt, carry):
        k0 = pl.multiple_of(kt * KEY_TILE, KEY_TILE)
        kt_keys = key_ref[kt]
        adm = ((k0 + col) >> CHUNK_SHIFT) <= q_chunk
        pieces = []
        for ch in range(LANE_CHUNKS):
            kc = kt_keys[:, ch * LANES:(ch + 1) * LANES]
            idx = k0 + ch * LANES + lane
            tie = jnp.where(kc == thr, jnp.where(idx < cut, 0.0, -jnp.inf), -jnp.inf)
            pieces.append(jnp.where(kc > thr, 0.0, tie))
        b = jnp.concatenate(pieces, axis=-1)
        bias_ref[kt] = jnp.where(adm, b, -jnp.inf)
        return carry

    lax.fori_loop(0, n_kt, bias_body, 0)

    for h in range(N_HEADS):
        m_ref[h] = jnp.full((Q_TILE, LANES), NEG_BIG, F32)
        acc_ref[h] = jnp.zeros((Q_TILE, LANES), F32)

    lo_kv = lax.broadcasted_iota(I32, (KEY_TILE, LANES), 1) < HEAD_DIM

    def attn_body(kt, carry):
        k0 = pl.multiple_of(kt * KEY_TILE, KEY_TILE)
        for hp in range(N_HEADS // 2):
            cols = slice(hp * LANES, (hp + 1) * LANES)
            k_t = k_ref[pl.ds(k0, KEY_TILE), cols]
            v_t = v_ref[pl.ds(k0, KEY_TILE), cols]
            one = jnp.ones_like(v_t)
            v_ext = (jnp.where(lo_kv, v_t, one), jnp.where(lo_kv, one, v_t))
            for par in range(2):
                h = 2 * hp + par
                s = _dot_nt(qm_ref[h], k_t) + bias_ref[kt]
                t_max = jnp.max(_fold_lanes(jnp.full((Q_TILE, LANES), -jnp.inf, F32), s, jnp.maximum),
                                axis=-1, keepdims=True)
                m_old = m_ref[h]
                m_new = jnp.maximum(m_old, t_max)
                alpha = jnp.exp2(m_old - m_new)
                e = jnp.concatenate(
                    [jnp.exp2(s[:, ch * LANES:(ch + 1) * LANES] - m_new) for ch in range(LANE_CHUNKS)],
                    axis=-1)
                acc_ref[h] = alpha * acc_ref[h] + _dot(e.astype(BF16), v_ext[par])
                m_ref[h] = m_new
        return carry

    lax.fori_loop(0, n_kt, attn_body, 0)

    for hp in range(N_HEADS // 2):
        acc_e = acc_ref[2 * hp]
        acc_o = acc_ref[2 * hp + 1]
        num = jnp.where(lo, acc_e, acc_o)
        den = pltpu.roll(jnp.where(lo, acc_o, acc_e), HEAD_DIM, 1)
        o_ref[:, hp * LANES:(hp + 1) * LANES] = (num / den).astype(BF16)


def _attn(p, kk, wi, batch, seq, topk):
    m = p.shape[0]
    nq = seq // Q_TILE
    n_kt = seq // KEY_TILE
    once = pl.Buffered(1)
    return pl.pallas_call(
        functools.partial(_attn_kernel, topk=topk),
        out_shape=jax.ShapeDtypeStruct((m, D_ATTN), BF16),
        grid=(batch, nq),
        in_specs=[
            pl.BlockSpec((Q_TILE, D_ATTN), lambda b, i: (b * nq + i, 5)),
            pl.BlockSpec((Q_TILE, D_ATTN), lambda b, i: (b * nq + i, 2)),
            pl.BlockSpec((seq, D_ATTN), lambda b, i: (b, 3), pipeline_mode=once),
            pl.BlockSpec((seq, D_ATTN), lambda b, i: (b, 4), pipeline_mode=once),
            pl.BlockSpec((seq, LANES), lambda b, i: (b, 0), pipeline_mode=once),
            pl.BlockSpec((Q_TILE, LANES), lambda b, i: (b * nq + i, 0)),
        ],
        out_specs=pl.BlockSpec((Q_TILE, D_ATTN), lambda b, i: (b * nq + i, 0)),
        scratch_shapes=[
            pltpu.VMEM((Q_SUBS, IDX_HEADS * Q_SUB, LANES), BF16),
            pltpu.VMEM((N_HEADS, Q_TILE, LANES), BF16),
            pltpu.VMEM((IDX_HEADS, Q_TILE, LANES), F32),
            pltpu.VMEM((n_kt, Q_TILE, KEY_TILE), F32),
            pltpu.VMEM((n_kt, Q_TILE, KEY_TILE), F32),
            pltpu.VMEM((N_HEADS, Q_TILE, LANES), F32),
            pltpu.VMEM((N_HEADS, Q_TILE, LANES), F32),
            pltpu.VMEM((Q_TILE, LANES), F32),
            pltpu.VMEM((Q_TILE, LANES), I32),
            pltpu.VMEM((Q_TILE, LANES), F32),
            pltpu.VMEM((Q_TILE, LANES), I32),
        ],
        compiler_params=_params(("arbitrary", "arbitrary")),
        name="attn",
    )(p, p, p, p, kk, wi)


MERGE_ROWS = 256


def _rms_rows(x, g):
    ms = jnp.mean(x * x, axis=-1, keepdims=True)
    return (x * lax.rsqrt(ms + EPS)) * g


def _merge_kernel(uc_ref, oa_ref, gc_ref, ga_ref, x_ref, gt_ref, gp_ref, g2_ref, sc2_ref, sh2_ref,
                  wc_ref, wa_ref, wo_ref, *rest):
    n_cast = (len(rest) - 2) // 2
    o_ref, h2_ref = rest[n_cast], rest[n_cast + 1]
    _cast_slabs(rest[:n_cast] + rest[n_cast + 2:])

    tm = x_ref.shape[0]
    gt = gt_ref[0]
    gp = gp_ref[...]
    g2 = g2_ref[...]
    sc2 = sc2_ref[0]
    sh2 = sh2_ref[0]

    def body(r, carry):
        r0 = pl.multiple_of(r * MERGE_ROWS, MERGE_ROWS)
        rows = pl.ds(r0, MERGE_ROWS)
        yc = _dot(uc_ref[rows, :], wc_ref[...])
        ya = _dot(oa_ref[rows, :], wa_ref[...])
        gc = _sigmoid(gc_ref[rows, :].astype(F32))
        ga = _sigmoid(ga_ref[rows, :].astype(F32))
        merged = (gc * yc + ga * ya).astype(BF16)
        mixed = _dot(merged, wo_ref[...])
        x1 = x_ref[rows, :] + gt * _rms_rows(mixed, gp)
        o_ref[rows, :] = x1
        h2_ref[rows, :] = _adaln_rows(x1, g2, sc2, sh2).astype(BF16)
        return carry

    lax.fori_loop(0, tm // MERGE_ROWS, body, 0)


def _cast_specs(weights, steps, index_map):
    in_specs, out_specs, out_shapes = [], [], []
    for w in weights:
        rows, cols = w.shape
        slab = rows // steps
        assert slab * steps == rows and slab % (2 * SUBLANES) == 0, (w.shape, steps)
        in_specs.append(pl.BlockSpec((slab, cols), index_map))
        out_specs.append(pl.BlockSpec((slab, cols), index_map))
        out_shapes.append(jax.ShapeDtypeStruct((rows, cols), BF16))
    return in_specs, out_specs, out_shapes


def _cast_slabs(refs):
    half = len(refs) // 2
    for src, dst in zip(refs[:half], refs[half:]):
        dst[...] = src[...].astype(BF16)


def _merge(uc, oa, p, x2, gt1, g_post, g_pre_next, sc_next, sh_next, wc, wa, wo, next_weights,
           seq, tm=256):
    m, d = x2.shape
    dc = uc.shape[1]
    per_b = seq // tm
    gate_blk = p.shape[1] // d - 2
    once = pl.Buffered(1)
    cast_in, cast_out, cast_shapes = _cast_specs(next_weights, m // tm, lambda i: (i, 0))
    return pl.pallas_call(
        _merge_kernel,
        out_shape=[jax.ShapeDtypeStruct((m, d), F32), jax.ShapeDtypeStruct((m, d), BF16)]
        + cast_shapes,
        grid=(m // tm,),
        in_specs=[
            pl.BlockSpec((tm, dc), lambda i: (i, 0)),
            pl.BlockSpec((tm, dc), lambda i: (i, 0)),
            pl.BlockSpec((tm, d), lambda i: (i, gate_blk)),
            pl.BlockSpec((tm, d), lambda i: (i, gate_blk + 1)),
            pl.BlockSpec((tm, d), lambda i: (i, 0)),
            pl.BlockSpec((1, 1, d), lambda i: (i // per_b, 0, 0)),
            pl.BlockSpec((1, d), lambda i: (0, 0)),
            pl.BlockSpec((1, d), lambda i: (0, 0)),
            pl.BlockSpec((1, 1, d), lambda i: (i // per_b, 0, 0)),
            pl.BlockSpec((1, 1, d), lambda i: (i // per_b, 0, 0)),
            pl.BlockSpec((dc, d), lambda i: (0, 0), pipeline_mode=once),
            pl.BlockSpec((dc, d), lambda i: (0, 0), pipeline_mode=once),
            pl.BlockSpec((d, d), lambda i: (0, 0), pipeline_mode=once),
        ] + cast_in,
        out_specs=[pl.BlockSpec((tm, d), lambda i: (i, 0)), pl.BlockSpec((tm, d), lambda i: (i, 0))]
        + cast_out,
        compiler_params=_params(("arbitrary",)),
        name="merge",
    )(uc, oa, p, p, x2, gt1, g_post, g_pre_next, sc_next, sh_next, wc, wa, wo, *next_weights)


FFN_ROWS = 512


def _ffn_kernel(x_ref, h_ref, gt_ref, gp_ref, wg_ref, wu_ref, wd_ref, o_ref, acc_ref):
    j = pl.program_id(1)
    tm = x_ref.shape[0]
    n_rows = tm // FFN_ROWS

    @pl.when(j == 0)
    def _():
        acc_ref[...] = jnp.zeros(acc_ref.shape, F32)

    for r in range(n_rows):
        rows = slice(r * FFN_ROWS, (r + 1) * FFN_ROWS)
        h = h_ref[rows, :]
        a = _dot(h, wg_ref[...])
        b = _dot(h, wu_ref[...])
        act = ((a * _sigmoid(a)) * b).astype(BF16)
        acc_ref[rows, :] = acc_ref[rows, :] + _dot(act, wd_ref[...])

    @pl.when(j == pl.num_programs(1) - 1)
    def _():
        gt = gt_ref[0]
        gp = gp_ref[...]

        def body(r, carry):
            rows = pl.ds(pl.multiple_of(r * ROW_CHUNK, ROW_CHUNK), ROW_CHUNK)
            o_ref[rows, :] = x_ref[rows, :] + gt * _rms_rows(acc_ref[rows, :], gp)
            return carry

        lax.fori_loop(0, tm // ROW_CHUNK, body, 0)


def _ffn(x2, h2, gt2, g_post, wg, wu, wd, seq, tm=512, tf=512):
    m, d = x2.shape
    dff = wg.shape[1]
    per_b = seq // tm
    return pl.pallas_call(
        _ffn_kernel,
        out_shape=jax.ShapeDtypeStruct((m, d), F32),
        grid=(m // tm, dff // tf),
        in_specs=[
            pl.BlockSpec((tm, d), lambda i, j: (i, 0)),
            pl.BlockSpec((tm, d), lambda i, j: (i, 0)),
            pl.BlockSpec((1, 1, d), lambda i, j: (i // per_b, 0, 0)),
            pl.BlockSpec((1, d), lambda i, j: (0, 0)),
            pl.BlockSpec((d, tf), lambda i, j: (0, j)),
            pl.BlockSpec((d, tf), lambda i, j: (0, j)),
            pl.BlockSpec((tf, d), lambda i, j: (j, 0)),
        ],
        out_specs=pl.BlockSpec((tm, d), lambda i, j: (i, 0)),
        scratch_shapes=[pltpu.VMEM((tm, d), F32)],
        compiler_params=_params(("arbitrary", "arbitrary")),
        name="ffn",
    )(x2, h2, gt2, g_post, wg, wu, wd)


def kernel(x, c, w_ada, b_ada, g_pre_mix, w_in, w_dw, b_dw, g_conv_ln, b_conv_ln, w_conv_out, w_attn_out, w_o, g_post_mix, g_pre_ffn, w_gate, w_up, w_down, g_post_ffn):
    batch, seq, d = x.shape
    depth = w_ada.shape[0]
    topk = min(TOPK_MAX, seq // 4)

    off_glu = 2 * D_CONV
    off_ki = off_glu + 3 * D_ATTN + IDX_HEADS * IDX_DIM
    off_wi = off_ki + IDX_DIM
    off_gate = off_wi + IDX_HEADS

    c8 = jnp.zeros((SUBLANES, d), F32).at[:batch].set(c)
    x2 = x.reshape(batch * seq, d)
    for l in range(depth):
        mod = _ada(c8, w_ada[l], b_ada[l][None, :])[:batch]
        sh1, sc1, gt1, sh2, sc2, gt2 = [t[:, None, :] for t in jnp.split(mod, 6, axis=-1)]

        w = w_in[l]
        w_bf = w.astype(BF16)
        w_gates = w_bf[:, off_gate:]
        w_ki = w_bf[:, off_ki:off_wi]
        w_wi = jnp.pad(w_bf[:, off_wi:off_gate], ((0, 0), (0, LANES - IDX_HEADS)))
        w_small = jnp.concatenate([w_ki, w_ki, w_wi], axis=1)

        p, kk, wi = _proj(x2, g_pre_mix[l][None, :], sc1, sh1, w_bf, w_gates, w_small, seq,
                          n_main_cols=off_ki)
        uc, wc_bf, wa_bf, wo_bf = _conv(
            p, w_dw[l], b_dw[l][None, :], g_conv_ln[l][None, :], b_conv_ln[l][None, :],
            (w_conv_out[l], w_attn_out[l], w_o[l]), batch, seq)
        oa = _attn(p, kk, wi, batch, seq, topk)
        x2, h2, wg_bf, wu_bf, wd_bf = _merge(
            uc, oa, p, x2, gt1, g_post_mix[l][None, :], g_pre_ffn[l][None, :], sc2, sh2,
            wc_bf, wa_bf, wo_bf, (w_gate[l], w_up[l], w_down[l]), seq)
        x2 = _ffn(x2, h2, gt2, g_post_ffn[l][None, :], wg_bf, wu_bf, wd_bf, seq)
    return x2.reshape(batch, seq, d)
```

```python
import functools

import jax
import jax.numpy as jnp
from jax import lax
from jax.experimental import pallas as pl
from jax.experimental.pallas import tpu as pltpu

F32 = jnp.float32
BF16 = jnp.bfloat16
I32 = jnp.int32

CHUNK = 64
CHUNK_SHIFT = 6
D_CONV = 1024
CONV_WIDTH = 31
N_HEADS = 16
HEAD_DIM = 64
D_ATTN = N_HEADS * HEAD_DIM
IDX_HEADS = 16
IDX_DIM = 64
TOPK_MAX = 256
EPS = 1e-6

LANES = 128
SUBLANES = 8
VMEM_LIMIT_BYTES = 56 * 1024 * 1024

LOG2_E = 1.4426950408889634
NEG_BIG = -1e30
INT_MIN = -(2 ** 31)
NEG_INF_KEY = (0xFF800000 ^ 0x7FFFFFFF) - 2 ** 32


def _sigmoid(x):
    return 1.0 / (1.0 + jnp.exp(-x))


def _dot(a, b):
    return jnp.dot(a, b, preferred_element_type=F32)


def _dot_nt(a, b):
    return lax.dot_general(a, b, (((1,), (1,)), ((), ())), preferred_element_type=F32)


def _params(sem):
    return pltpu.CompilerParams(dimension_semantics=sem, vmem_limit_bytes=VMEM_LIMIT_BYTES)


def _ada_kernel(c_ref, w_ref, b_ref, o_ref):
    c = c_ref[...]
    ca = (c * _sigmoid(c)).astype(BF16)
    o_ref[...] = _dot(ca, w_ref[...].astype(BF16)) + b_ref[...]


def _ada(c8, w, b, tn=1024):
    rows, d = c8.shape
    n = w.shape[1]
    return pl.pallas_call(
        _ada_kernel,
        out_shape=jax.ShapeDtypeStruct((rows, n), F32),
        grid=(n // tn,),
        in_specs=[
            pl.BlockSpec((rows, d), lambda j: (0, 0)),
            pl.BlockSpec((d, tn), lambda j: (0, j)),
            pl.BlockSpec((1, tn), lambda j: (0, j)),
        ],
        out_specs=pl.BlockSpec((rows, tn), lambda j: (0, j)),
        compiler_params=_params(("arbitrary",)),
        name="ada",
    )(c8, w, b)


ROW_CHUNK = 128


def _adaln_rows(x, g, sc, sh):
    ms = jnp.mean(x * x, axis=-1, keepdims=True)
    y = (x * lax.rsqrt(ms + EPS)) * g
    return y * (1.0 + sc) + sh


def _proj_kernel(x_ref, g_ref, sc_ref, sh_ref, w_ref, wg_ref, ws_ref, o_ref, kk_ref, wi_ref, h_ref,
                 *, n_main):
    j = pl.program_id(1)
    tm = x_ref.shape[0]

    @pl.when(j == 0)
    def _():
        g = g_ref[...]
        sc = sc_ref[0]
        sh = sh_ref[0]

        def body(r, carry):
            r0 = pl.multiple_of(r * ROW_CHUNK, ROW_CHUNK)
            h = _adaln_rows(x_ref[pl.ds(r0, ROW_CHUNK), :], g, sc, sh).astype(BF16)
            h_ref[pl.ds(r0, ROW_CHUNK), :] = h
            small = _dot(h, ws_ref[...])
            kk_ref[pl.ds(r0, ROW_CHUNK), :] = small[:, :LANES].astype(BF16)
            wi_ref[pl.ds(r0, ROW_CHUNK), :] = small[:, LANES:]
            return carry

        lax.fori_loop(0, tm // ROW_CHUNK, body, 0)

    @pl.when(j < n_main)
    def _():
        o_ref[...] = _dot(h_ref[...], w_ref[...]).astype(BF16)

    @pl.when(j >= n_main)
    def _():
        o_ref[...] = _dot(h_ref[...], wg_ref[...]).astype(BF16)


def _proj(x2, g, sc, sh, w_main, w_gates, w_small, seq, n_main_cols, tm=1024, tn=1024):
    m, d = x2.shape
    n_main = n_main_cols // tn
    n = n_main_cols + w_gates.shape[1]
    per_b = seq // tm
    return pl.pallas_call(
        functools.partial(_proj_kernel, n_main=n_main),
        out_shape=(
            jax.ShapeDtypeStruct((m, n), BF16),
            jax.ShapeDtypeStruct((m, LANES), BF16),
            jax.ShapeDtypeStruct((m, LANES), F32),
        ),
        grid=(m // tm, n // tn),
        in_specs=[
            pl.BlockSpec((tm, d), lambda i, j: (i, 0)),
            pl.BlockSpec((1, d), lambda i, j: (0, 0)),
            pl.BlockSpec((1, 1, d), lambda i, j: (i // per_b, 0, 0)),
            pl.BlockSpec((1, 1, d), lambda i, j: (i // per_b, 0, 0)),
            pl.BlockSpec((d, tn), lambda i, j: (0, jnp.minimum(j, n_main - 1))),
            pl.BlockSpec((d, tn), lambda i, j: (0, jnp.maximum(j - n_main, 0))),
            pl.BlockSpec((d, 2 * LANES), lambda i, j: (0, 0)),
        ],
        out_specs=(
            pl.BlockSpec((tm, tn), lambda i, j: (i, j)),
            pl.BlockSpec((tm, LANES), lambda i, j: (i, 0)),
            pl.BlockSpec((tm, LANES), lambda i, j: (i, 0)),
        ),
        scratch_shapes=[pltpu.VMEM((tm, d), BF16)],
        compiler_params=_params(("arbitrary", "arbitrary")),
        name="proj",
    )(x2, g, sc, sh, w_main, w_gates, w_small)


CONV_HALO = 32
CONV_ROWS = 64
NORM_ROWS = 256


def _conv_kernel(a_ref, gt_ref, w_ref, bdw_ref, gln_ref, bln_ref, *rest):
    n_cast = (len(rest) - 5) // 2
    o_ref = rest[n_cast]
    u_ref, cv_ref, wb_ref, sh_ref = rest[2 * n_cast + 1:]
    _cast_slabs(rest[:n_cast] + rest[n_cast + 1:2 * n_cast + 1])

    s = pl.program_id(1)
    ts = a_ref.shape[0]
    dc = a_ref.shape[1]

    @pl.when(s == 0)
    def _():
        u_ref[0:CONV_HALO, :] = jnp.zeros((CONV_HALO, dc), F32)

    @pl.when(s > 0)
    def _():
        u_ref[0:CONV_HALO, :] = u_ref[ts:ts + CONV_HALO, :]

    def glu_body(r, carry):
        r0 = pl.multiple_of(r * CONV_ROWS, CONV_ROWS)
        a = a_ref[pl.ds(r0, CONV_ROWS), :].astype(F32)
        g = gt_ref[pl.ds(r0, CONV_ROWS), :].astype(F32)
        u_ref[pl.ds(CONV_HALO + r0, CONV_ROWS), :] = a * _sigmoid(g)
        return carry

    lax.fori_loop(0, ts // CONV_ROWS, glu_body, 0)

    for jtap in range(CONV_WIDTH):
        wb_ref[jtap * SUBLANES:(jtap + 1) * SUBLANES, :] = jnp.broadcast_to(
            w_ref[jtap:jtap + 1, :], (SUBLANES, dc))

    base = CONV_HALO - (CONV_WIDTH - 1)

    taps = {}
    for jtap in range(CONV_WIDTH):
        a, b = divmod(base + jtap, SUBLANES)
        taps.setdefault(b, []).append((a, jtap))

    def conv_body(r, carry):
        r0 = pl.multiple_of(r * CONV_ROWS, CONV_ROWS)
        for lc in range(dc // LANES):
            cols = slice(lc * LANES, (lc + 1) * LANES)
            win = u_ref[pl.ds(r0, CONV_ROWS + CONV_HALO), cols]
            for b, group in taps.items():
                span = CONV_ROWS + SUBLANES * max(a for a, _ in group)
                sh_ref[lc, b, 0:span, :] = win[b:b + span, :]
            acc = jnp.zeros((CONV_ROWS, LANES), F32)
            for b, group in taps.items():
                for a, jtap in group:
                    w_tap = wb_ref[jtap * SUBLANES:(jtap + 1) * SUBLANES, cols]
                    w_rows = jnp.concatenate([w_tap] * (CONV_ROWS // SUBLANES), axis=0)
                    acc = acc + sh_ref[lc, b, SUBLANES * a:SUBLANES * a + CONV_ROWS, :] * w_rows
            cv_ref[pl.ds(r0, CONV_ROWS), cols] = acc + bdw_ref[:, cols]
        return carry

    lax.fori_loop(0, ts // CONV_ROWS, conv_body, 0)

    def norm_body(r, carry):
        r0 = pl.multiple_of(r * NORM_ROWS, NORM_ROWS)
        xr = cv_ref[pl.ds(r0, NORM_ROWS), :]
        mu = jnp.mean(xr, axis=-1, keepdims=True)
        xc = xr - mu
        var = jnp.mean(xc * xc, axis=-1, keepdims=True)
        y = (xc * lax.rsqrt(var + EPS)) * gln_ref[...] + bln_ref[...]
        o_ref[pl.ds(r0, NORM_ROWS), :] = (y * _sigmoid(y)).astype(BF16)
        return carry

    lax.fori_loop(0, ts // NORM_ROWS, norm_body, 0)


def _conv(p, w_dw, b_dw, g_ln, b_ln, later_weights, batch, seq, ts=512):
    m = p.shape[0]
    dc = w_dw.shape[1]
    per_b = seq // ts
    cast_in, cast_out, cast_shapes = _cast_specs(
        later_weights, batch * per_b, lambda b, s: (b * per_b + s, 0))
    return pl.pallas_call(
        _conv_kernel,
        out_shape=[jax.ShapeDtypeStruct((m, dc), BF16)] + cast_shapes,
        grid=(batch, per_b),
        in_specs=[
            pl.BlockSpec((ts, dc), lambda b, s: (b * per_b + s, 0)),
            pl.BlockSpec((ts, dc), lambda b, s: (b * per_b + s, 1)),
            pl.BlockSpec((CONV_WIDTH, dc), lambda b, s: (0, 0)),
            pl.BlockSpec((1, dc), lambda b, s: (0, 0)),
            pl.BlockSpec((1, dc), lambda b, s: (0, 0)),
            pl.BlockSpec((1, dc), lambda b, s: (0, 0)),
        ] + cast_in,
        out_specs=[pl.BlockSpec((ts, dc), lambda b, s: (b * per_b + s, 0))] + cast_out,
        scratch_shapes=[
            pltpu.VMEM((ts + CONV_HALO, dc), F32),
            pltpu.VMEM((ts, dc), F32),
            pltpu.VMEM((CONV_WIDTH * SUBLANES, dc), F32),
            pltpu.VMEM((dc // LANES, SUBLANES, CONV_ROWS + CONV_HALO, LANES), F32),
        ],
        compiler_params=_params(("arbitrary", "arbitrary")),
        name="conv",
    )(p, p, w_dw, b_dw, g_ln, b_ln, *later_weights)


Q_TILE = 256
Q_SUB = 128
Q_SUBS = Q_TILE // Q_SUB
KEY_TILE = 256
KEY_SUB = 256
SEL_ROWS = 64
LANE_CHUNKS = KEY_TILE // LANES
INDEX_BITS = 13


def _fold_lanes(acc, t, op):
    for ch in range(t.shape[1] // LANES):
        acc = op(acc, t[:, ch * LANES:(ch + 1) * LANES])
    return acc


def _attn_kernel(qi_ref, q_ref, k_ref, v_ref, kk_ref, wi_ref, o_ref,
                 qim_ref, qm_ref, wrep_ref, key_ref, bias_ref, m_ref, acc_ref, thr_ref,
                 cut_ref, cand_ref, idx_ref, *, topk):
    i = pl.program_id(1)
    p0 = i * Q_TILE
    n_kt = (p0 + Q_TILE + KEY_TILE - 1) // KEY_TILE
    idx_scale = (IDX_DIM ** -0.5) * (IDX_HEADS ** -0.5)
    attn_scale = HEAD_DIM ** -0.5

    lane = lax.broadcasted_iota(I32, (Q_TILE, LANES), 1)
    lo = lane < HEAD_DIM
    lo_sub = lax.broadcasted_iota(I32, (Q_SUB, LANES), 1) < HEAD_DIM
    row = lax.broadcasted_iota(I32, (Q_TILE, KEY_TILE), 0)
    col = lax.broadcasted_iota(I32, (Q_TILE, KEY_TILE), 1)
    q_chunk = (p0 + row) >> CHUNK_SHIFT

    wi = wi_ref[...]
    for hp in range(N_HEADS // 2):
        cols = slice(hp * LANES, (hp + 1) * LANES)
        q_p = q_ref[:, cols].astype(F32) * (attn_scale * LOG2_E)
        qm_ref[2 * hp] = jnp.where(lo, q_p, 0.0).astype(BF16)
        qm_ref[2 * hp + 1] = jnp.where(lo, 0.0, q_p).astype(BF16)
        for rs in range(Q_SUBS):
            qi_p = qi_ref[rs * Q_SUB:(rs + 1) * Q_SUB, cols]
            zero_i = jnp.zeros_like(qi_p)
            qim_ref[rs, (2 * hp) * Q_SUB:(2 * hp + 1) * Q_SUB, :] = jnp.where(lo_sub, qi_p, zero_i)
            qim_ref[rs, (2 * hp + 1) * Q_SUB:(2 * hp + 2) * Q_SUB, :] = jnp.where(lo_sub, zero_i, qi_p)
    for h in range(IDX_HEADS):
        wrep_ref[h] = jnp.broadcast_to(wi[:, h:h + 1], (Q_TILE, LANES))

    sub_row = lax.broadcasted_iota(I32, (Q_SUB, KEY_SUB), 0)
    sub_col = lax.broadcasted_iota(I32, (Q_SUB, KEY_SUB), 1)

    def score_body(kt, carry):
        k0 = pl.multiple_of(kt * KEY_TILE, KEY_TILE)
        for ks in range(KEY_TILE // KEY_SUB):
            kk_s = kk_ref[pl.ds(k0 + ks * KEY_SUB, KEY_SUB), :]
            for rs in range(Q_SUBS):
                rows = slice(rs * Q_SUB, (rs + 1) * Q_SUB)
                rel = _dot_nt(qim_ref[rs], kk_s)
                acc = [jnp.zeros((Q_SUB, LANES), F32) for _ in range(KEY_SUB // LANES)]
                for h in range(IDX_HEADS):
                    w_h = wrep_ref[h, rows, :]
                    for ch in range(KEY_SUB // LANES):
                        r = rel[h * Q_SUB:(h + 1) * Q_SUB, ch * LANES:(ch + 1) * LANES]
                        acc[ch] = acc[ch] + jnp.maximum(r, 0.0) * w_h
                score = jnp.concatenate(acc, axis=-1) * idx_scale
                adm = ((k0 + ks * KEY_SUB + sub_col) >> CHUNK_SHIFT) <= (
                    (p0 + rs * Q_SUB + sub_row) >> CHUNK_SHIFT)
                key_ref[kt, rows, ks * KEY_SUB:(ks + 1) * KEY_SUB] = jnp.where(adm, score, -jnp.inf)
        return carry

    lax.fori_loop(0, n_kt, score_body, 0)

    kf = float(topk)

    def key_to_float(key_s):
        bits = key_s ^ ((key_s >> 31) & 0x7FFFFFFF)
        return jnp.where(key_s <= NEG_INF_KEY, -jnp.inf, pltpu.bitcast(bits, F32))

    lane_s = lax.broadcasted_iota(I32, (SEL_ROWS, LANES), 1)

    row_groups = [slice(rg * SEL_ROWS, (rg + 1) * SEL_ROWS) for rg in range(Q_TILE // SEL_ROWS)]

    def count(indicator):
        def body(kt, cs):
            out = []
            for rows, c in zip(row_groups, cs):
                for ch in range(LANE_CHUNKS):
                    kc = key_ref[kt, rows, ch * LANES:(ch + 1) * LANES]
                    c = c + indicator(kc, kt * KEY_TILE + ch * LANES + lane_s, rows)
                out.append(c)
            return tuple(out)

        zero = jnp.zeros((SEL_ROWS, LANES), F32)
        cs = lax.fori_loop(0, n_kt, body, tuple(zero for _ in row_groups))
        return jnp.sum(jnp.concatenate(cs, axis=0), axis=-1, keepdims=True)

    def bisect_body(it, prefix):
        bit = lax.shift_left(jnp.int32(1), jnp.int32(31) - it)
        cand_u = prefix | bit
        cand_ref[...] = key_to_float(cand_u ^ INT_MIN)
        cnt = count(lambda kc, _, rows: jnp.where(kc >= cand_ref[rows, :], 1.0, 0.0))
        return jnp.where(cnt >= kf, cand_u, prefix)

    prefix = lax.fori_loop(0, 32, bisect_body, jnp.zeros((Q_TILE, LANES), I32))
    thr = key_to_float(prefix ^ INT_MIN)
    thr_ref[...] = thr

    n_gt = count(lambda kc, _, rows: jnp.where(kc > thr_ref[rows, :], 1.0, 0.0))
    n_ge = count(lambda kc, _, rows: jnp.where(kc >= thr_ref[rows, :], 1.0, 0.0))
    want = kf - n_gt
    cut_ref[...] = jnp.full((Q_TILE, LANES), 2 ** INDEX_BITS, I32)
    tied = jnp.where(thr[:, :1] == -jnp.inf, 0.0, jnp.where(n_ge > kf, 1.0, 0.0))
    surplus = jnp.max(tied)

    @pl.when(surplus > 0.0)
    def _():
        def cut_body(it, cut):
            bit = lax.shift_left(jnp.int32(1), jnp.int32(INDEX_BITS - 1) - it)
            cand = cut | bit
            idx_ref[...] = cand
            cnt = count(lambda kc, idx, rows: jnp.where(
                kc == thr_ref[rows, :], jnp.where(idx < idx_ref[rows, :], 1.0, 0.0), 0.0))
            return jnp.where(cnt <= want, cand, cut)

        cut_ref[...] = lax.fori_loop(0, INDEX_BITS, cut_body, jnp.zeros((Q_TILE, LANES), I32))

    cut = cut_ref[...]

    def bias_body(kt, carry):
        k0 = pl.multiple_of(kt * KEY_TILE, KEY_TILE)
        kt_keys = key_ref[kt]
        adm = ((k0 + col) >> CHUNK_SHIFT) <= q_chunk
        pieces = []
        for ch in range(LANE_CHUNKS):
            kc = kt_keys[:, ch * LANES:(ch + 1) * LANES]
            idx = k0 + ch * LANES + lane
            tie = jnp.where(kc == thr, jnp.where(idx < cut, 0.0, -jnp.inf), -jnp.inf)
            pieces.append(jnp.where(kc > thr, 0.0, tie))
        b = jnp.concatenate(pieces, axis=-1)
        bias_ref[kt] = jnp.where(adm, b, -jnp.inf)
        return carry

    lax.fori_loop(0, n_kt, bias_body, 0)

    for h in range(N_HEADS):
        m_ref[h] = jnp.full((Q_TILE, LANES), NEG_BIG, F32)
        acc_ref[h] = jnp.zeros((Q_TILE, LANES), F32)

    lo_kv = lax.broadcasted_iota(I32, (KEY_TILE, LANES), 1) < HEAD_DIM

    def attn_body(kt, carry):
        k0 = pl.multiple_of(kt * KEY_TILE, KEY_TILE)
        for hp in range(N_HEADS // 2):
            cols = slice(hp * LANES, (hp + 1) * LANES)
            k_t = k_ref[pl.ds(k0, KEY_TILE), cols]
            v_t = v_ref[pl.ds(k0, KEY_TILE), cols]
            one = jnp.ones_like(v_t)
            v_ext = (jnp.where(lo_kv, v_t, one), jnp.where(lo_kv, one, v_t))
            for par in range(2):
                h = 2 * hp + par
                s = _dot_nt(qm_ref[h], k_t) + bias_ref[kt]
                t_max = jnp.max(_fold_lanes(s[:, :LANES], s[:, LANES:], jnp.maximum),
                                axis=-1, keepdims=True)
                m_old = m_ref[h]
                m_new = jnp.maximum(m_old, t_max)
                alpha = jnp.exp2(m_old - m_new)
                e = jnp.concatenate(
                    [jnp.exp2(s[:, ch * LANES:(ch + 1) * LANES] - m_new) for ch in range(LANE_CHUNKS)],
                    axis=-1)
                acc_ref[h] = alpha * acc_ref[h] + _dot(e.astype(BF16), v_ext[par])
                m_ref[h] = m_new
        return carry

    lax.fori_loop(0, n_kt, attn_body, 0)

    for hp in range(N_HEADS // 2):
        acc_e = acc_ref[2 * hp]
        acc_o = acc_ref[2 * hp + 1]
        num = jnp.where(lo, acc_e, acc_o)
        den = pltpu.roll(jnp.where(lo, acc_o, acc_e), HEAD_DIM, 1)
        o_ref[:, hp * LANES:(hp + 1) * LANES] = (num / den).astype(BF16)


def _attn(p, kk, wi, batch, seq, topk):
    m = p.shape[0]
    nq = seq // Q_TILE
    n_kt = seq // KEY_TILE
    once = pl.Buffered(1)
    return pl.pallas_call(
        functools.partial(_attn_kernel, topk=topk),
        out_shape=jax.ShapeDtypeStruct((m, D_ATTN), BF16),
        grid=(batch, nq),
        in_specs=[
            pl.BlockSpec((Q_TILE, D_ATTN), lambda b, i: (b * nq + i, 5)),
            pl.BlockSpec((Q_TILE, D_ATTN), lambda b, i: (b * nq + i, 2)),
            pl.BlockSpec((seq, D_ATTN), lambda b, i: (b, 3), pipeline_mode=once),
            pl.BlockSpec((seq, D_ATTN), lambda b, i: (b, 4), pipeline_mode=once),
            pl.BlockSpec((seq, LANES), lambda b, i: (b, 0), pipeline_mode=once),
            pl.BlockSpec((Q_TILE, LANES), lambda b, i: (b * nq + i, 0)),
        ],
        out_specs=pl.BlockSpec((Q_TILE, D_ATTN), lambda b, i: (b * nq + i, 0)),
        scratch_shapes=[
            pltpu.VMEM((Q_SUBS, IDX_HEADS * Q_SUB, LANES), BF16),
            pltpu.VMEM((N_HEADS, Q_TILE, LANES), BF16),
            pltpu.VMEM((IDX_HEADS, Q_TILE, LANES), F32),
            pltpu.VMEM((n_kt, Q_TILE, KEY_TILE), F32),
            pltpu.VMEM((n_kt, Q_TILE, KEY_TILE), F32),
            pltpu.VMEM((N_HEADS, Q_TILE, LANES), F32),
            pltpu.VMEM((N_HEADS, Q_TILE, LANES), F32),
            pltpu.VMEM((Q_TILE, LANES), F32),
            pltpu.VMEM((Q_TILE, LANES), I32),
            pltpu.VMEM((Q_TILE, LANES), F32),
            pltpu.VMEM((Q_TILE, LANES), I32),
        ],
        compiler_params=_params(("arbitrary", "arbitrary")),
        name="attn",
    )(p, p, p, p, kk, wi)


MERGE_ROWS = 256


def _rms_rows(x, g):
    ms = jnp.mean(x * x, axis=-1, keepdims=True)
    return (x * lax.rsqrt(ms + EPS)) * g


def _merge_kernel(uc_ref, oa_ref, gc_ref, ga_ref, x_ref, gt_ref, gp_ref, g2_ref, sc2_ref, sh2_ref,
                  wc_ref, wa_ref, wo_ref, *rest):
    n_cast = (len(rest) - 2) // 2
    o_ref, h2_ref = rest[n_cast], rest[n_cast + 1]
    _cast_slabs(rest[:n_cast] + rest[n_cast + 2:])

    tm = x_ref.shape[0]
    gt = gt_ref[0]
    gp = gp_ref[...]
    g2 = g2_ref[...]
    sc2 = sc2_ref[0]
    sh2 = sh2_ref[0]

    def body(r, carry):
        r0 = pl.multiple_of(r * MERGE_ROWS, MERGE_ROWS)
        rows = pl.ds(r0, MERGE_ROWS)
        yc = _dot(uc_ref[rows, :], wc_ref[...])
        ya = _dot(oa_ref[rows, :], wa_ref[...])
        gc = _sigmoid(gc_ref[rows, :].astype(F32))
        ga = _sigmoid(ga_ref[rows, :].astype(F32))
        merged = (gc * yc + ga * ya).astype(BF16)
        mixed = _dot(merged, wo_ref[...])
        x1 = x_ref[rows, :] + gt * _rms_rows(mixed, gp)
        o_ref[rows, :] = x1
        h2_ref[rows, :] = _adaln_rows(x1, g2, sc2, sh2).astype(BF16)
        return carry

    lax.fori_loop(0, tm // MERGE_ROWS, body, 0)


def _cast_specs(weights, steps, index_map):
    in_specs, out_specs, out_shapes = [], [], []
    for w in weights:
        rows, cols = w.shape
        slab = rows // steps
        assert slab * steps == rows and slab % (2 * SUBLANES) == 0, (w.shape, steps)
        in_specs.append(pl.BlockSpec((slab, cols), index_map))
        out_specs.append(pl.BlockSpec((slab, cols), index_map))
        out_shapes.append(jax.ShapeDtypeStruct((rows, cols), BF16))
    return in_specs, out_specs, out_shapes


def _cast_slabs(refs):
    half = len(refs) // 2
    for src, dst in zip(refs[:half], refs[half:]):
        dst[...] = src[...].astype(BF16)


def _merge(uc, oa, p, x2, gt1, g_post, g_pre_next, sc_next, sh_next, wc, wa, wo, next_weights,
           seq, tm=256):
    m, d = x2.shape
    dc = uc.shape[1]
    per_b = seq // tm
    gate_blk = p.shape[1] // d - 2
    once = pl.Buffered(1)
    cast_in, cast_out, cast_shapes = _cast_specs(next_weights, m // tm, lambda i: (i, 0))
    return pl.pallas_call(
        _merge_kernel,
        out_shape=[jax.ShapeDtypeStruct((m, d), F32), jax.ShapeDtypeStruct((m, d), BF16)]
        + cast_shapes,
        grid=(m // tm,),
        in_specs=[
            pl.BlockSpec((tm, dc), lambda i: (i, 0)),
            pl.BlockSpec((tm, dc), lambda i: (i, 0)),
            pl.BlockSpec((tm, d), lambda i: (i, gate_blk)),
            pl.BlockSpec((tm, d), lambda i: (i, gate_blk + 1)),
            pl.BlockSpec((tm, d), lambda i: (i, 0)),
            pl.BlockSpec((1, 1, d), lambda i: (i // per_b, 0, 0)),
            pl.BlockSpec((1, d), lambda i: (0, 0)),
            pl.BlockSpec((1, d), lambda i: (0, 0)),
            pl.BlockSpec((1, 1, d), lambda i: (i // per_b, 0, 0)),
            pl.BlockSpec((1, 1, d), lambda i: (i // per_b, 0, 0)),
            pl.BlockSpec((dc, d), lambda i: (0, 0), pipeline_mode=once),
            pl.BlockSpec((dc, d), lambda i: (0, 0), pipeline_mode=once),
            pl.BlockSpec((d, d), lambda i: (0, 0), pipeline_mode=once),
        ] + cast_in,
        out_specs=[pl.BlockSpec((tm, d), lambda i: (i, 0)), pl.BlockSpec((tm, d), lambda i: (i, 0))]
        + cast_out,
        compiler_params=_params(("arbitrary",)),
        name="merge",
    )(uc, oa, p, p, x2, gt1, g_post, g_pre_next, sc_next, sh_next, wc, wa, wo, *next_weights)


FFN_ROWS = 512


def _ffn_kernel(x_ref, h_ref, gt_ref, gp_ref, wg_ref, wu_ref, wd_ref, o_ref, acc_ref):
    j = pl.program_id(1)
    tm = x_ref.shape[0]
    n_rows = tm // FFN_ROWS

    @pl.when(j == 0)
    def _():
        acc_ref[...] = jnp.zeros(acc_ref.shape, F32)

    for r in range(n_rows):
        rows = slice(r * FFN_ROWS, (r + 1) * FFN_ROWS)
        h = h_ref[rows, :]
        a = _dot(h, wg_ref[...])
        b = _dot(h, wu_ref[...])
        act = ((a * _sigmoid(a)) * b).astype(BF16)
        acc_ref[rows, :] = acc_ref[rows, :] + _dot(act, wd_ref[...])

    @pl.when(j == pl.num_programs(1) - 1)
    def _():
        gt = gt_ref[0]
        gp = gp_ref[...]

        def body(r, carry):
            rows = pl.ds(pl.multiple_of(r * ROW_CHUNK, ROW_CHUNK), ROW_CHUNK)
            o_ref[rows, :] = x_ref[rows, :] + gt * _rms_rows(acc_ref[rows, :], gp)
            return carry

        lax.fori_loop(0, tm // ROW_CHUNK, body, 0)


def _ffn(x2, h2, gt2, g_post, wg, wu, wd, seq, tm=512, tf=512):
    m, d = x2.shape
    dff = wg.shape[1]
    per_b = seq // tm
    return pl.pallas_call(
        _ffn_kernel,
        out_shape=jax.ShapeDtypeStruct((m, d), F32),
        grid=(m // tm, dff // tf),
        in_specs=[
            pl.BlockSpec((tm, d), lambda i, j: (i, 0)),
            pl.BlockSpec((tm, d), lambda i, j: (i, 0)),
            pl.BlockSpec((1, 1, d), lambda i, j: (i // per_b, 0, 0)),
            pl.BlockSpec((1, d), lambda i, j: (0, 0)),
            pl.BlockSpec((d, tf), lambda i, j: (0, j)),
            pl.BlockSpec((d, tf), lambda i, j: (0, j)),
            pl.BlockSpec((tf, d), lambda i, j: (j, 0)),
        ],
        out_specs=pl.BlockSpec((tm, d), lambda i, j: (i, 0)),
        scratch_shapes=[pltpu.VMEM((tm, d), F32)],
        compiler_params=_params(("arbitrary", "arbitrary")),
        name="ffn",
    )(x2, h2, gt2, g_post, wg, wu, wd)


def kernel(x, c, w_ada, b_ada, g_pre_mix, w_in, w_dw, b_dw, g_conv_ln, b_conv_ln, w_conv_out, w_attn_out, w_o, g_post_mix, g_pre_ffn, w_gate, w_up, w_down, g_post_ffn):
    batch, seq, d = x.shape
    depth = w_ada.shape[0]
    topk = min(TOPK_MAX, seq // 4)

    off_glu = 2 * D_CONV
    off_ki = off_glu + 3 * D_ATTN + IDX_HEADS * IDX_DIM
    off_wi = off_ki + IDX_DIM
    off_gate = off_wi + IDX_HEADS

    c8 = jnp.zeros((SUBLANES, d), F32).at[:batch].set(c)
    x2 = x.reshape(batch * seq, d)
    for l in range(depth):
        mod = _ada(c8, w_ada[l], b_ada[l][None, :])[:batch]
        sh1, sc1, gt1, sh2, sc2, gt2 = [t[:, None, :] for t in jnp.split(mod, 6, axis=-1)]

        w = w_in[l]
        w_bf = w.astype(BF16)
        w_gates = w_bf[:, off_gate:]
        w_ki = w_bf[:, off_ki:off_wi]
        w_wi = jnp.pad(w_bf[:, off_wi:off_gate], ((0, 0), (0, LANES - IDX_HEADS)))
        w_small = jnp.concatenate([w_ki, w_ki, w_wi], axis=1)

        p, kk, wi = _proj(x2, g_pre_mix[l][None, :], sc1, sh1, w_bf, w_gates, w_small, seq,
                          n_main_cols=off_ki)
        uc, wc_bf, wa_bf, wo_bf = _conv(
            p, w_dw[l], b_dw[l][None, :], g_conv_ln[l][None, :], b_conv_ln[l][None, :],
            (w_conv_out[l], w_attn_out[l], w_o[l]), batch, seq)
        oa = _attn(p, kk, wi, batch, seq, topk)
        x2, h2, wg_bf, wu_bf, wd_bf = _merge(
            uc, oa, p, x2, gt1, g_post_mix[l][None, :], g_pre_ffn[l][None, :], sc2, sh2,
            wc_bf, wa_bf, wo_bf, (w_gate[l], w_up[l], w_down[l]), seq)
        x2 = _ffn(x2, h2, gt2, g_post_ffn[l][None, :], wg_bf, wu_bf, wd_bf, seq)
    return x2.reshape(batch, seq, d)
```

```python
import functools

import jax
import jax.numpy as jnp
from jax import lax
from jax.experimental import pallas as pl
from jax.experimental.pallas import tpu as pltpu

F32 = jnp.float32
BF16 = jnp.bfloat16
I32 = jnp.int32

CHUNK = 64
CHUNK_SHIFT = 6
D_CONV = 1024
CONV_WIDTH = 31
N_HEADS = 16
HEAD_DIM = 64
D_ATTN = N_HEADS * HEAD_DIM
IDX_HEADS = 16
IDX_DIM = 64
TOPK_MAX = 256
EPS = 1e-6

LANES = 128
SUBLANES = 8
VMEM_LIMIT_BYTES = 56 * 1024 * 1024

LOG2_E = 1.4426950408889634
NEG_BIG = -1e30
INT_MIN = -(2 ** 31)
NEG_INF_KEY = (0xFF800000 ^ 0x7FFFFFFF) - 2 ** 32


def _sigmoid(x):
    return 1.0 / (1.0 + jnp.exp(-x))


def _dot(a, b):
    return jnp.dot(a, b, preferred_element_type=F32)


def _dot_nt(a, b):
    return lax.dot_general(a, b, (((1,), (1,)), ((), ())), preferred_element_type=F32)


def _params(sem):
    return pltpu.CompilerParams(dimension_semantics=sem, vmem_limit_bytes=VMEM_LIMIT_BYTES)


def _ada_kernel(c_ref, w_ref, b_ref, o_ref):
    c = c_ref[...]
    ca = (c * _sigmoid(c)).astype(BF16)
    o_ref[...] = _dot(ca, w_ref[...].astype(BF16)) + b_ref[...]


def _ada(c8, w, b, tn=1024):
    rows, d = c8.shape
    n = w.shape[1]
    return pl.pallas_call(
        _ada_kernel,
        out_shape=jax.ShapeDtypeStruct((rows, n), F32),
        grid=(n // tn,),
        in_specs=[
            pl.BlockSpec((rows, d), lambda j: (0, 0)),
            pl.BlockSpec((d, tn), lambda j: (0, j)),
            pl.BlockSpec((1, tn), lambda j: (0, j)),
        ],
        out_specs=pl.BlockSpec((rows, tn), lambda j: (0, j)),
        compiler_params=_params(("arbitrary",)),
        name="ada",
    )(c8, w, b)


ROW_CHUNK = 128


def _adaln_rows(x, g, sc, sh):
    ms = jnp.mean(x * x, axis=-1, keepdims=True)
    y = (x * lax.rsqrt(ms + EPS)) * g
    return y * (1.0 + sc) + sh


def _proj_kernel(x_ref, g_ref, sc_ref, sh_ref, w_ref, wg_ref, ws_ref, o_ref, kk_ref, wi_ref, h_ref,
                 *, n_main):
    j = pl.program_id(1)
    tm = x_ref.shape[0]

    @pl.when(j == 0)
    def _():
        g = g_ref[...]
        sc = sc_ref[0]
        sh = sh_ref[0]

        def body(r, carry):
            r0 = pl.multiple_of(r * ROW_CHUNK, ROW_CHUNK)
            h = _adaln_rows(x_ref[pl.ds(r0, ROW_CHUNK), :], g, sc, sh).astype(BF16)
            h_ref[pl.ds(r0, ROW_CHUNK), :] = h
            small = _dot(h, ws_ref[...])
            kk_ref[pl.ds(r0, ROW_CHUNK), :] = small[:, :LANES].astype(BF16)
            wi_ref[pl.ds(r0, ROW_CHUNK), :] = small[:, LANES:]
            return carry

        lax.fori_loop(0, tm // ROW_CHUNK, body, 0)

    @pl.when(j < n_main)
    def _():
        o_ref[...] = _dot(h_ref[...], w_ref[...]).astype(BF16)

    @pl.when(j >= n_main)
    def _():
        o_ref[...] = _dot(h_ref[...], wg_ref[...]).astype(BF16)


def _proj(x2, g, sc, sh, w_main, w_gates, w_small, seq, n_main_cols, tm=1024, tn=1024):
    m, d = x2.shape
    n_main = n_main_cols // tn
    n = n_main_cols + w_gates.shape[1]
    per_b = seq // tm
    return pl.pallas_call(
        functools.partial(_proj_kernel, n_main=n_main),
        out_shape=(
            jax.ShapeDtypeStruct((m, n), BF16),
            jax.ShapeDtypeStruct((m, LANES), BF16),
            jax.ShapeDtypeStruct((m, LANES), F32),
        ),
        grid=(m // tm, n // tn),
        in_specs=[
            pl.BlockSpec((tm, d), lambda i, j: (i, 0)),
            pl.BlockSpec((1, d), lambda i, j: (0, 0)),
            pl.BlockSpec((1, 1, d), lambda i, j: (i // per_b, 0, 0)),
            pl.BlockSpec((1, 1, d), lambda i, j: (i // per_b, 0, 0)),
            pl.BlockSpec((d, tn), lambda i, j: (0, jnp.minimum(j, n_main - 1))),
            pl.BlockSpec((d, tn), lambda i, j: (0, jnp.maximum(j - n_main, 0))),
            pl.BlockSpec((d, 2 * LANES), lambda i, j: (0, 0)),
        ],
        out_specs=(
            pl.BlockSpec((tm, tn), lambda i, j: (i, j)),
            pl.BlockSpec((tm, LANES), lambda i, j: (i, 0)),
            pl.BlockSpec((tm, LANES), lambda i, j: (i, 0)),
        ),
        scratch_shapes=[pltpu.VMEM((tm, d), BF16)],
        compiler_params=_params(("arbitrary", "arbitrary")),
        name="proj",
    )(x2, g, sc, sh, w_main, w_gates, w_small)


CONV_HALO = 32
CONV_ROWS = 64
NORM_ROWS = 256


def _conv_kernel(a_ref, gt_ref, w_ref, bdw_ref, gln_ref, bln_ref, *rest):
    n_cast = (len(rest) - 5) // 2
    o_ref = rest[n_cast]
    u_ref, cv_ref, wb_ref, sh_ref = rest[2 * n_cast + 1:]
    _cast_slabs(rest[:n_cast] + rest[n_cast + 1:2 * n_cast + 1])

    s = pl.program_id(1)
    ts = a_ref.shape[0]
    dc = a_ref.shape[1]

    @pl.when(s == 0)
    def _():
        u_ref[0:CONV_HALO, :] = jnp.zeros((CONV_HALO, dc), F32)

    @pl.when(s > 0)
    def _():
        u_ref[0:CONV_HALO, :] = u_ref[ts:ts + CONV_HALO, :]

    def glu_body(r, carry):
        r0 = pl.multiple_of(r * CONV_ROWS, CONV_ROWS)
        a = a_ref[pl.ds(r0, CONV_ROWS), :].astype(F32)
        g = gt_ref[pl.ds(r0, CONV_ROWS), :].astype(F32)
        u_ref[pl.ds(CONV_HALO + r0, CONV_ROWS), :] = a * _sigmoid(g)
        return carry

    lax.fori_loop(0, ts // CONV_ROWS, glu_body, 0)

    for jtap in range(CONV_WIDTH):
        wb_ref[jtap * SUBLANES:(jtap + 1) * SUBLANES, :] = jnp.broadcast_to(
            w_ref[jtap:jtap + 1, :], (SUBLANES, dc))

    base = CONV_HALO - (CONV_WIDTH - 1)

    taps = {}
    for jtap in range(CONV_WIDTH):
        a, b = divmod(base + jtap, SUBLANES)
        taps.setdefault(b, []).append((a, jtap))

    def conv_body(r, carry):
        r0 = pl.multiple_of(r * CONV_ROWS, CONV_ROWS)
        for lc in range(dc // LANES):
            cols = slice(lc * LANES, (lc + 1) * LANES)
            win = u_ref[pl.ds(r0, CONV_ROWS + CONV_HALO), cols]
            for b, group in taps.items():
                span = CONV_ROWS + SUBLANES * max(a for a, _ in group)
                sh_ref[lc, b, 0:span, :] = win[b:b + span, :]
            acc = jnp.zeros((CONV_ROWS, LANES), F32)
            for b, group in taps.items():
                for a, jtap in group:
                    w_tap = wb_ref[jtap * SUBLANES:(jtap + 1) * SUBLANES, cols]
                    w_rows = jnp.concatenate([w_tap] * (CONV_ROWS // SUBLANES), axis=0)
                    acc = acc + sh_ref[lc, b, SUBLANES * a:SUBLANES * a + CONV_ROWS, :] * w_rows
            cv_ref[pl.ds(r0, CONV_ROWS), cols] = acc + bdw_ref[:, cols]
        return carry

    lax.fori_loop(0, ts // CONV_ROWS, conv_body, 0)

    def norm_body(r, carry):
        r0 = pl.multiple_of(r * NORM_ROWS, NORM_ROWS)
        xr = cv_ref[pl.ds(r0, NORM_ROWS), :]
        mu = jnp.mean(xr, axis=-1, keepdims=True)
        xc = xr - mu
        var = jnp.mean(xc * xc, axis=-1, keepdims=True)
        y = (xc * lax.rsqrt(var + EPS)) * gln_ref[...] + bln_ref[...]
        o_ref[pl.ds(r0, NORM_ROWS), :] = (y * _sigmoid(y)).astype(BF16)
        return carry

    lax.fori_loop(0, ts // NORM_ROWS, norm_body, 0)


def _conv(p, w_dw, b_dw, g_ln, b_ln, later_weights, batch, seq, ts=512):
    m = p.shape[0]
    dc = w_dw.shape[1]
    per_b = seq // ts
    cast_in, cast_out, cast_shapes = _cast_specs(
        later_weights, batch * per_b, lambda b, s: (b * per_b + s, 0))
    return pl.pallas_call(
        _conv_kernel,
        out_shape=[jax.ShapeDtypeStruct((m, dc), BF16)] + cast_shapes,
        grid=(batch, per_b),
        in_specs=[
            pl.BlockSpec((ts, dc), lambda b, s: (b * per_b + s, 0)),
            pl.BlockSpec((ts, dc), lambda b, s: (b * per_b + s, 1)),
            pl.BlockSpec((CONV_WIDTH, dc), lambda b, s: (0, 0)),
            pl.BlockSpec((1, dc), lambda b, s: (0, 0)),
            pl.BlockSpec((1, dc), lambda b, s: (0, 0)),
            pl.BlockSpec((1, dc), lambda b, s: (0, 0)),
        ] + cast_in,
        out_specs=[pl.BlockSpec((ts, dc), lambda b, s: (b * per_b + s, 0))] + cast_out,
        scratch_shapes=[
            pltpu.VMEM((ts + CONV_HALO, dc), F32),
            pltpu.VMEM((ts, dc), F32),
            pltpu.VMEM((CONV_WIDTH * SUBLANES, dc), F32),
            pltpu.VMEM((dc // LANES, SUBLANES, CONV_ROWS + CONV_HALO, LANES), F32),
        ],
        compiler_params=_params(("arbitrary", "arbitrary")),
        name="conv",
    )(p, p, w_dw, b_dw, g_ln, b_ln, *later_weights)


Q_TILE = 256
Q_SUB = 128
Q_SUBS = Q_TILE // Q_SUB
KEY_TILE = 256
KEY_SUB = 256
SEL_ROWS = 64
LANE_CHUNKS = KEY_TILE // LANES
INDEX_BITS = 13


def _fold_lanes(acc, t, op):
    for ch in range(t.shape[1] // LANES):
        acc = op(acc, t[:, ch * LANES:(ch + 1) * LANES])
    return acc


def _attn_kernel(qi_ref, q_ref, k_ref, v_ref, kk_ref, wi_ref, o_ref,
                 qim_ref, qm_ref, wrep_ref, key_ref, bias_ref, m_ref, acc_ref, thr_ref,
                 cut_ref, cand_ref, idx_ref, *, topk):
    i = pl.program_id(1)
    p0 = i * Q_TILE
    n_kt = (p0 + Q_TILE + KEY_TILE - 1) // KEY_TILE
    idx_scale = (IDX_DIM ** -0.5) * (IDX_HEADS ** -0.5)
    attn_scale = HEAD_DIM ** -0.5

    lane = lax.broadcasted_iota(I32, (Q_TILE, LANES), 1)
    lo = lane < HEAD_DIM
    lo_sub = lax.broadcasted_iota(I32, (Q_SUB, LANES), 1) < HEAD_DIM
    row = lax.broadcasted_iota(I32, (Q_TILE, KEY_TILE), 0)
    col = lax.broadcasted_iota(I32, (Q_TILE, KEY_TILE), 1)
    q_chunk = (p0 + row) >> CHUNK_SHIFT

    wi = wi_ref[...]
    for hp in range(N_HEADS // 2):
        cols = slice(hp * LANES, (hp + 1) * LANES)
        q_p = q_ref[:, cols].astype(F32) * (attn_scale * LOG2_E)
        qm_ref[2 * hp] = jnp.where(lo, q_p, 0.0).astype(BF16)
        qm_ref[2 * hp + 1] = jnp.where(lo, 0.0, q_p).astype(BF16)
        for rs in range(Q_SUBS):
            qi_p = qi_ref[rs * Q_SUB:(rs + 1) * Q_SUB, cols]
            zero_i = jnp.zeros_like(qi_p)
            qim_ref[rs, (2 * hp) * Q_SUB:(2 * hp + 1) * Q_SUB, :] = jnp.where(lo_sub, qi_p, zero_i)
            qim_ref[rs, (2 * hp + 1) * Q_SUB:(2 * hp + 2) * Q_SUB, :] = jnp.where(lo_sub, zero_i, qi_p)
    for h in range(IDX_HEADS):
        wrep_ref[h] = jnp.broadcast_to(wi[:, h:h + 1], (Q_TILE, LANES))

    sub_row = lax.broadcasted_iota(I32, (Q_SUB, KEY_SUB), 0)
    sub_col = lax.broadcasted_iota(I32, (Q_SUB, KEY_SUB), 1)

    def score_body(kt, carry):
        k0 = pl.multiple_of(kt * KEY_TILE, KEY_TILE)
        for ks in range(KEY_TILE // KEY_SUB):
            kk_s = kk_ref[pl.ds(k0 + ks * KEY_SUB, KEY_SUB), :]
            for rs in range(Q_SUBS):
                rows = slice(rs * Q_SUB, (rs + 1) * Q_SUB)
                rel = _dot_nt(qim_ref[rs], kk_s)
                acc = [jnp.zeros((Q_SUB, LANES), F32) for _ in range(KEY_SUB // LANES)]
                for h in range(IDX_HEADS):
                    w_h = wrep_ref[h, rows, :]
                    for ch in range(KEY_SUB // LANES):
                        r = rel[h * Q_SUB:(h + 1) * Q_SUB, ch * LANES:(ch + 1) * LANES]
                        acc[ch] = acc[ch] + jnp.maximum(r, 0.0) * w_h
                score = jnp.concatenate(acc, axis=-1) * idx_scale
                adm = ((k0 + ks * KEY_SUB + sub_col) >> CHUNK_SHIFT) <= (
                    (p0 + rs * Q_SUB + sub_row) >> CHUNK_SHIFT)
                key_ref[kt, rows, ks * KEY_SUB:(ks + 1) * KEY_SUB] = jnp.where(adm, score, -jnp.inf)
        return carry

    lax.fori_loop(0, n_kt, score_body, 0)

    kf = float(topk)

    def key_to_float(key_s):
        bits = key_s ^ ((key_s >> 31) & 0x7FFFFFFF)
        return jnp.where(key_s <= NEG_INF_KEY, -jnp.inf, pltpu.bitcast(bits, F32))

    lane_s = lax.broadcasted_iota(I32, (SEL_ROWS, LANES), 1)

    row_groups = [slice(rg * SEL_ROWS, (rg + 1) * SEL_ROWS) for rg in range(Q_TILE // SEL_ROWS)]

    def count(indicator):
        def body(kt, cs):
            out = []
            for rows, c in zip(row_groups, cs):
                for ch in range(LANE_CHUNKS):
                    kc = key_ref[kt, rows, ch * LANES:(ch + 1) * LANES]
                    c = c + indicator(kc, kt * KEY_TILE + ch * LANES + lane_s, rows)
                out.append(c)
            return tuple(out)

        zero = jnp.zeros((SEL_ROWS, LANES), F32)
        cs = lax.fori_loop(0, n_kt, body, tuple(zero for _ in row_groups))
        return jnp.sum(jnp.concatenate(cs, axis=0), axis=-1, keepdims=True)

    def bisect_body(it, prefix):
        bit = lax.shift_left(jnp.int32(1), jnp.int32(31) - it)
        cand_u = prefix | bit
        cand_ref[...] = key_to_float(cand_u ^ INT_MIN)
        cnt = count(lambda kc, _, rows: jnp.where(kc >= cand_ref[rows, :], 1.0, 0.0))
        return jnp.where(cnt >= kf, cand_u, prefix)

    prefix = lax.fori_loop(0, 32, bisect_body, jnp.zeros((Q_TILE, LANES), I32))
    thr = key_to_float(prefix ^ INT_MIN)
    thr_ref[...] = thr

    n_ge = count(lambda kc, _, rows: jnp.where(kc >= thr_ref[rows, :], 1.0, 0.0))
    cut_ref[...] = jnp.full((Q_TILE, LANES), 2 ** INDEX_BITS, I32)
    tied = jnp.where(thr[:, :1] == -jnp.inf, 0.0, jnp.where(n_ge > kf, 1.0, 0.0))
    surplus = jnp.max(tied)

    @pl.when(surplus > 0.0)
    def _():
        n_gt = count(lambda kc, _, rows: jnp.where(kc > thr_ref[rows, :], 1.0, 0.0))
        want = kf - n_gt

        def cut_body(it, cut):
            bit = lax.shift_left(jnp.int32(1), jnp.int32(INDEX_BITS - 1) - it)
            cand = cut | bit
            idx_ref[...] = cand
            cnt = count(lambda kc, idx, rows: jnp.where(
                kc == thr_ref[rows, :], jnp.where(idx < idx_ref[rows, :], 1.0, 0.0), 0.0))
            return jnp.where(cnt <= want, cand, cut)

        cut_ref[...] = lax.fori_loop(0, INDEX_BITS, cut_body, jnp.zeros((Q_TILE, LANES), I32))

    cut = cut_ref[...]

    def bias_body(kt, carry):
        k0 = pl.multiple_of(kt * KEY_TILE, KEY_TILE)
        kt_keys = key_ref[kt]
        adm = ((k0 + col) >> CHUNK_SHIFT) <= q_chunk
        pieces = []
        for ch in range(LANE_CHUNKS):
            kc = kt_keys[:, ch * LANES:(ch + 1) * LANES]
            idx = k0 + ch * LANES + lane
            tie = jnp.where(kc == thr, jnp.where(idx < cut, 0.0, -jnp.inf), -jnp.inf)
            pieces.append(jnp.where(kc > thr, 0.0, tie))
        b = jnp.concatenate(pieces, axis=-1)
        bias_ref[kt] = jnp.where(adm, b, -jnp.inf)
        return carry

    lax.fori_loop(0, n_kt, bias_body, 0)

    for h in range(N_HEADS):
        m_ref[h] = jnp.full((Q_TILE, LANES), NEG_BIG, F32)
        acc_ref[h] = jnp.zeros((Q_TILE, LANES), F32)

    lo_kv = lax.broadcasted_iota(I32, (KEY_TILE, LANES), 1) < HEAD_DIM

    def attn_body(kt, carry):
        k0 = pl.multiple_of(kt * KEY_TILE, KEY_TILE)
        for hp in range(N_HEADS // 2):
            cols = slice(hp * LANES, (hp + 1) * LANES)
            k_t = k_ref[pl.ds(k0, KEY_TILE), cols]
            v_t = v_ref[pl.ds(k0, KEY_TILE), cols]
            one = jnp.ones_like(v_t)
            v_ext = (jnp.where(lo_kv, v_t, one), jnp.where(lo_kv, one, v_t))
            for par in range(2):
                h = 2 * hp + par
                s = _dot_nt(qm_ref[h], k_t) + bias_ref[kt]
                t_max = jnp.max(_fold_lanes(s[:, :LANES], s[:, LANES:], jnp.maximum),
                                axis=-1, keepdims=True)
                m_old = m_ref[h]
                m_new = jnp.maximum(m_old, t_max)
                alpha = jnp.exp2(m_old - m_new)
                e = jnp.concatenate(
                    [jnp.exp2(s[:, ch * LANES:(ch + 1) * LANES] - m_new) for ch in range(LANE_CHUNKS)],
                    axis=-1)
                acc_ref[h] = alpha * acc_ref[h] + _dot(e.astype(BF16), v_ext[par])
                m_ref[h] = m_new
        return carry

    lax.fori_loop(0, n_kt, attn_body, 0)

    for hp in range(N_HEADS // 2):
        acc_e = acc_ref[2 * hp]
        acc_o = acc_ref[2 * hp + 1]
        num = jnp.where(lo, acc_e, acc_o)
        den = pltpu.roll(jnp.where(lo, acc_o, acc_e), HEAD_DIM, 1)
        o_ref[:, hp * LANES:(hp + 1) * LANES] = (num / den).astype(BF16)


def _attn(p, kk, wi, batch, seq, topk):
    m = p.shape[0]
    nq = seq // Q_TILE
    n_kt = seq // KEY_TILE
    once = pl.Buffered(1)
    return pl.pallas_call(
        functools.partial(_attn_kernel, topk=topk),
        out_shape=jax.ShapeDtypeStruct((m, D_ATTN), BF16),
        grid=(batch, nq),
        in_specs=[
            pl.BlockSpec((Q_TILE, D_ATTN), lambda b, i: (b * nq + i, 5)),
            pl.BlockSpec((Q_TILE, D_ATTN), lambda b, i: (b * nq + i, 2)),
            pl.BlockSpec((seq, D_ATTN), lambda b, i: (b, 3), pipeline_mode=once),
            pl.BlockSpec((seq, D_ATTN), lambda b, i: (b, 4), pipeline_mode=once),
            pl.BlockSpec((seq, LANES), lambda b, i: (b, 0), pipeline_mode=once),
            pl.BlockSpec((Q_TILE, LANES), lambda b, i: (b * nq + i, 0)),
        ],
        out_specs=pl.BlockSpec((Q_TILE, D_ATTN), lambda b, i: (b * nq + i, 0)),
        scratch_shapes=[
            pltpu.VMEM((Q_SUBS, IDX_HEADS * Q_SUB, LANES), BF16),
            pltpu.VMEM((N_HEADS, Q_TILE, LANES), BF16),
            pltpu.VMEM((IDX_HEADS, Q_TILE, LANES), F32),
            pltpu.VMEM((n_kt, Q_TILE, KEY_TILE), F32),
            pltpu.VMEM((n_kt, Q_TILE, KEY_TILE), F32),
            pltpu.VMEM((N_HEADS, Q_TILE, LANES), F32),
            pltpu.VMEM((N_HEADS, Q_TILE, LANES), F32),
            pltpu.VMEM((Q_TILE, LANES), F32),
            pltpu.VMEM((Q_TILE, LANES), I32),
            pltpu.VMEM((Q_TILE, LANES), F32),
            pltpu.VMEM((Q_TILE, LANES), I32),
        ],
        compiler_params=_params(("arbitrary", "arbitrary")),
        name="attn",
    )(p, p, p, p, kk, wi)


MERGE_ROWS = 256


def _rms_rows(x, g):
    ms = jnp.mean(x * x, axis=-1, keepdims=True)
    return (x * lax.rsqrt(ms + EPS)) * g


def _merge_kernel(uc_ref, oa_ref, gc_ref, ga_ref, x_ref, gt_ref, gp_ref, g2_ref, sc2_ref, sh2_ref,
                  wc_ref, wa_ref, wo_ref, *rest):
    n_cast = (len(rest) - 2) // 2
    o_ref, h2_ref = rest[n_cast], rest[n_cast + 1]
    _cast_slabs(rest[:n_cast] + rest[n_cast + 2:])

    tm = x_ref.shape[0]
    gt = gt_ref[0]
    gp = gp_ref[...]
    g2 = g2_ref[...]
    sc2 = sc2_ref[0]
    sh2 = sh2_ref[0]

    def body(r, carry):
        r0 = pl.multiple_of(r * MERGE_ROWS, MERGE_ROWS)
        rows = pl.ds(r0, MERGE_ROWS)
        yc = _dot(uc_ref[rows, :], wc_ref[...])
        ya = _dot(oa_ref[rows, :], wa_ref[...])
        gc = _sigmoid(gc_ref[rows, :].astype(F32))
        ga = _sigmoid(ga_ref[rows, :].astype(F32))
        merged = (gc * yc + ga * ya).astype(BF16)
        mixed = _dot(merged, wo_ref[...])
        x1 = x_ref[rows, :] + gt * _rms_rows(mixed, gp)
        o_ref[rows, :] = x1
        h2_ref[rows, :] = _adaln_rows(x1, g2, sc2, sh2).astype(BF16)
        return carry

    lax.fori_loop(0, tm // MERGE_ROWS, body, 0)


def _cast_specs(weights, steps, index_map):
    in_specs, out_specs, out_shapes = [], [], []
    for w in weights:
        rows, cols = w.shape
        slab = rows // steps
        assert slab * steps == rows and slab % (2 * SUBLANES) == 0, (w.shape, steps)
        in_specs.append(pl.BlockSpec((slab, cols), index_map))
        out_specs.append(pl.BlockSpec((slab, cols), index_map))
        out_shapes.append(jax.ShapeDtypeStruct((rows, cols), BF16))
    return in_specs, out_specs, out_shapes


def _cast_slabs(refs):
    half = len(refs) // 2
    for src, dst in zip(refs[:half], refs[half:]):
        dst[...] = src[...].astype(BF16)


def _merge(uc, oa, p, x2, gt1, g_post, g_pre_next, sc_next, sh_next, wc, wa, wo, next_weights,
           seq, tm=256):
    m, d = x2.shape
    dc = uc.shape[1]
    per_b = seq // tm
    gate_blk = p.shape[1] // d - 2
    once = pl.Buffered(1)
    cast_in, cast_out, cast_shapes = _cast_specs(next_weights, m // tm, lambda i: (i, 0))
    return pl.pallas_call(
        _merge_kernel,
        out_shape=[jax.ShapeDtypeStruct((m, d), F32), jax.ShapeDtypeStruct((m, d), BF16)]
        + cast_shapes,
        grid=(m // tm,),
        in_specs=[
            pl.BlockSpec((tm, dc), lambda i: (i, 0)),
            pl.BlockSpec((tm, dc), lambda i: (i, 0)),
            pl.BlockSpec((tm, d), lambda i: (i, gate_blk)),
            pl.BlockSpec((tm, d), lambda i: (i, gate_blk + 1)),
            pl.BlockSpec((tm, d), lambda i: (i, 0)),
            pl.BlockSpec((1, 1, d), lambda i: (i // per_b, 0, 0)),
            pl.BlockSpec((1, d), lambda i: (0, 0)),
            pl.BlockSpec((1, d), lambda i: (0, 0)),
            pl.BlockSpec((1, 1, d), lambda i: (i // per_b, 0, 0)),
            pl.BlockSpec((1, 1, d), lambda i: (i // per_b, 0, 0)),
            pl.BlockSpec((dc, d), lambda i: (0, 0), pipeline_mode=once),
            pl.BlockSpec((dc, d), lambda i: (0, 0), pipeline_mode=once),
            pl.BlockSpec((d, d), lambda i: (0, 0), pipeline_mode=once),
        ] + cast_in,
        out_specs=[pl.BlockSpec((tm, d), lambda i: (i, 0)), pl.BlockSpec((tm, d), lambda i: (i, 0))]
        + cast_out,
        compiler_params=_params(("arbitrary",)),
        name="merge",
    )(uc, oa, p, p, x2, gt1, g_post, g_pre_next, sc_next, sh_next, wc, wa, wo, *next_weights)


FFN_ROWS = 512


def _ffn_kernel(x_ref, h_ref, gt_ref, gp_ref, wg_ref, wu_ref, wd_ref, o_ref, acc_ref):
    j = pl.program_id(1)
    tm = x_ref.shape[0]
    n_rows = tm // FFN_ROWS

    @pl.when(j == 0)
    def _():
        acc_ref[...] = jnp.zeros(acc_ref.shape, F32)

    for r in range(n_rows):
        rows = slice(r * FFN_ROWS, (r + 1) * FFN_ROWS)
        h = h_ref[rows, :]
        a = _dot(h, wg_ref[...])
        b = _dot(h, wu_ref[...])
        act = ((a * _sigmoid(a)) * b).astype(BF16)
        acc_ref[rows, :] = acc_ref[rows, :] + _dot(act, wd_ref[...])

    @pl.when(j == pl.num_programs(1) - 1)
    def _():
        gt = gt_ref[0]
        gp = gp_ref[...]

        def body(r, carry):
            rows = pl.ds(pl.multiple_of(r * ROW_CHUNK, ROW_CHUNK), ROW_CHUNK)
            o_ref[rows, :] = x_ref[rows, :] + gt * _rms_rows(acc_ref[rows, :], gp)
            return carry

        lax.fori_loop(0, tm // ROW_CHUNK, body, 0)


def _ffn(x2, h2, gt2, g_post, wg, wu, wd, seq, tm=512, tf=512):
    m, d = x2.shape
    dff = wg.shape[1]
    per_b = seq // tm
    return pl.pallas_call(
        _ffn_kernel,
        out_shape=jax.ShapeDtypeStruct((m, d), F32),
        grid=(m // tm, dff // tf),
        in_specs=[
            pl.BlockSpec((tm, d), lambda i, j: (i, 0)),
            pl.BlockSpec((tm, d), lambda i, j: (i, 0)),
            pl.BlockSpec((1, 1, d), lambda i, j: (i // per_b, 0, 0)),
            pl.BlockSpec((1, d), lambda i, j: (0, 0)),
            pl.BlockSpec((d, tf), lambda i, j: (0, j)),
            pl.BlockSpec((d, tf), lambda i, j: (0, j)),
            pl.BlockSpec((tf, d), lambda i, j: (j, 0)),
        ],
        out_specs=pl.BlockSpec((tm, d), lambda i, j: (i, 0)),
        scratch_shapes=[pltpu.VMEM((tm, d), F32)],
        compiler_params=_params(("arbitrary", "arbitrary")),
        name="ffn",
    )(x2, h2, gt2, g_post, wg, wu, wd)


def kernel(x, c, w_ada, b_ada, g_pre_mix, w_in, w_dw, b_dw, g_conv_ln, b_conv_ln, w_conv_out, w_attn_out, w_o, g_post_mix, g_pre_ffn, w_gate, w_up, w_down, g_post_ffn):
    batch, seq, d = x.shape
    depth = w_ada.shape[0]
    topk = min(TOPK_MAX, seq // 4)

    off_glu = 2 * D_CONV
    off_ki = off_glu + 3 * D_ATTN + IDX_HEADS * IDX_DIM
    off_wi = off_ki + IDX_DIM
    off_gate = off_wi + IDX_HEADS

    c8 = jnp.zeros((SUBLANES, d), F32).at[:batch].set(c)
    x2 = x.reshape(batch * seq, d)
    for l in range(depth):
        mod = _ada(c8, w_ada[l], b_ada[l][None, :])[:batch]
        sh1, sc1, gt1, sh2, sc2, gt2 = [t[:, None, :] for t in jnp.split(mod, 6, axis=-1)]

        w = w_in[l]
        w_bf = w.astype(BF16)
        w_gates = w_bf[:, off_gate:]
        w_ki = w_bf[:, off_ki:off_wi]
        w_wi = jnp.pad(w_bf[:, off_wi:off_gate], ((0, 0), (0, LANES - IDX_HEADS)))
        w_small = jnp.concatenate([w_ki, w_ki, w_wi], axis=1)

        p, kk, wi = _proj(x2, g_pre_mix[l][None, :], sc1, sh1, w_bf, w_gates, w_small, seq,
                          n_main_cols=off_ki)
        uc, wc_bf, wa_bf, wo_bf = _conv(
            p, w_dw[l], b_dw[l][None, :], g_conv_ln[l][None, :], b_conv_ln[l][None, :],
            (w_conv_out[l], w_attn_out[l], w_o[l]), batch, seq)
        oa = _attn(p, kk, wi, batch, seq, topk)
        x2, h2, wg_bf, wu_bf, wd_bf = _merge(
            uc, oa, p, x2, gt1, g_post_mix[l][None, :], g_pre_ffn[l][None, :], sc2, sh2,
            wc_bf, wa_bf, wo_bf, (w_gate[l], w_up[l], w_down[l]), seq)
        x2 = _ffn(x2, h2, gt2, g_post_ffn[l][None, :], wg_bf, wu_bf, wd_bf, seq)
    return x2.reshape(batch, seq, d)
```

```python
import functools

import jax
import jax.numpy as jnp
from jax import lax
from jax.experimental import pallas as pl
from jax.experimental.pallas import tpu as pltpu

F32 = jnp.float32
BF16 = jnp.bfloat16
I32 = jnp.int32

CHUNK = 64
CHUNK_SHIFT = 6
D_CONV = 1024
CONV_WIDTH = 31
N_HEADS = 16
HEAD_DIM = 64
D_ATTN = N_HEADS * HEAD_DIM
IDX_HEADS = 16
IDX_DIM = 64
TOPK_MAX = 256
EPS = 1e-6

LANES = 128
SUBLANES = 8
VMEM_LIMIT_BYTES = 56 * 1024 * 1024

LOG2_E = 1.4426950408889634
NEG_BIG = -1e30
INT_MIN = -(2 ** 31)
NEG_INF_KEY = (0xFF800000 ^ 0x7FFFFFFF) - 2 ** 32


def _sigmoid(x):
    return 1.0 / (1.0 + jnp.exp(-x))


def _dot(a, b):
    return jnp.dot(a, b, preferred_element_type=F32)


def _dot_nt(a, b):
    return lax.dot_general(a, b, (((1,), (1,)), ((), ())), preferred_element_type=F32)


def _params(sem):
    return pltpu.CompilerParams(dimension_semantics=sem, vmem_limit_bytes=VMEM_LIMIT_BYTES)


def _ada_kernel(c_ref, w_ref, b_ref, o_ref):
    c = c_ref[...]
    ca = (c * _sigmoid(c)).astype(BF16)
    o_ref[...] = _dot(ca, w_ref[...].astype(BF16)) + b_ref[...]


def _ada(c8, w, b, tn=1024):
    rows, d = c8.shape
    n = w.shape[1]
    return pl.pallas_call(
        _ada_kernel,
        out_shape=jax.ShapeDtypeStruct((rows, n), F32),
        grid=(n // tn,),
        in_specs=[
            pl.BlockSpec((rows, d), lambda j: (0, 0)),
            pl.BlockSpec((d, tn), lambda j: (0, j)),
            pl.BlockSpec((1, tn), lambda j: (0, j)),
        ],
        out_specs=pl.BlockSpec((rows, tn), lambda j: (0, j)),
        compiler_params=_params(("arbitrary",)),
        name="ada",
    )(c8, w, b)


ROW_CHUNK = 128


def _adaln_rows(x, g, sc, sh):
    ms = jnp.mean(x * x, axis=-1, keepdims=True)
    y = (x * lax.rsqrt(ms + EPS)) * g
    return y * (1.0 + sc) + sh


def _proj_kernel(x_ref, g_ref, sc_ref, sh_ref, w_ref, wg_ref, ws_ref, o_ref, kk_ref, wi_ref, h_ref,
                 *, n_main):
    j = pl.program_id(1)
    tm = x_ref.shape[0]

    @pl.when(j == 0)
    def _():
        g = g_ref[...]
        sc = sc_ref[0]
        sh = sh_ref[0]

        def body(r, carry):
            r0 = pl.multiple_of(r * ROW_CHUNK, ROW_CHUNK)
            h = _adaln_rows(x_ref[pl.ds(r0, ROW_CHUNK), :], g, sc, sh).astype(BF16)
            h_ref[pl.ds(r0, ROW_CHUNK), :] = h
            small = _dot(h, ws_ref[...])
            kk_ref[pl.ds(r0, ROW_CHUNK), :] = small[:, :LANES].astype(BF16)
            wi_ref[pl.ds(r0, ROW_CHUNK), :] = small[:, LANES:]
            return carry

        lax.fori_loop(0, tm // ROW_CHUNK, body, 0)

    @pl.when(j < n_main)
    def _():
        o_ref[...] = _dot(h_ref[...], w_ref[...]).astype(BF16)

    @pl.when(j >= n_main)
    def _():
        o_ref[...] = _dot(h_ref[...], wg_ref[...]).astype(BF16)


def _proj(x2, g, sc, sh, w_main, w_gates, w_small, seq, n_main_cols, tm=1024, tn=1024):
    m, d = x2.shape
    n_main = n_main_cols // tn
    n = n_main_cols + w_gates.shape[1]
    per_b = seq // tm
    return pl.pallas_call(
        functools.partial(_proj_kernel, n_main=n_main),
        out_shape=(
            jax.ShapeDtypeStruct((m, n), BF16),
            jax.ShapeDtypeStruct((m, LANES), BF16),
            jax.ShapeDtypeStruct((m, LANES), F32),
        ),
        grid=(m // tm, n // tn),
        in_specs=[
            pl.BlockSpec((tm, d), lambda i, j: (i, 0)),
            pl.BlockSpec((1, d), lambda i, j: (0, 0)),
            pl.BlockSpec((1, 1, d), lambda i, j: (i // per_b, 0, 0)),
            pl.BlockSpec((1, 1, d), lambda i, j: (i // per_b, 0, 0)),
            pl.BlockSpec((d, tn), lambda i, j: (0, jnp.minimum(j, n_main - 1))),
            pl.BlockSpec((d, tn), lambda i, j: (0, jnp.maximum(j - n_main, 0))),
            pl.BlockSpec((d, 2 * LANES), lambda i, j: (0, 0)),
        ],
        out_specs=(
            pl.BlockSpec((tm, tn), lambda i, j: (i, j)),
            pl.BlockSpec((tm, LANES), lambda i, j: (i, 0)),
            pl.BlockSpec((tm, LANES), lambda i, j: (i, 0)),
        ),
        scratch_shapes=[pltpu.VMEM((tm, d), BF16)],
        compiler_params=_params(("arbitrary", "arbitrary")),
        name="proj",
    )(x2, g, sc, sh, w_main, w_gates, w_small)


CONV_HALO = 32
CONV_ROWS = 64
NORM_ROWS = 256


def _conv_kernel(a_ref, gt_ref, w_ref, bdw_ref, gln_ref, bln_ref, *rest):
    n_cast = (len(rest) - 5) // 2
    o_ref = rest[n_cast]
    u_ref, cv_ref, wb_ref, sh_ref = rest[2 * n_cast + 1:]
    _cast_slabs(rest[:n_cast] + rest[n_cast + 1:2 * n_cast + 1])

    s = pl.program_id(1)
    ts = a_ref.shape[0]
    dc = a_ref.shape[1]

    @pl.when(s == 0)
    def _():
        u_ref[0:CONV_HALO, :] = jnp.zeros((CONV_HALO, dc), F32)

    @pl.when(s > 0)
    def _():
        u_ref[0:CONV_HALO, :] = u_ref[ts:ts + CONV_HALO, :]

    def glu_body(r, carry):
        r0 = pl.multiple_of(r * CONV_ROWS, CONV_ROWS)
        a = a_ref[pl.ds(r0, CONV_ROWS), :].astype(F32)
        g = gt_ref[pl.ds(r0, CONV_ROWS), :].astype(F32)
        u_ref[pl.ds(CONV_HALO + r0, CONV_ROWS), :] = a * _sigmoid(g)
        return carry

    lax.fori_loop(0, ts // CONV_ROWS, glu_body, 0)

    for jtap in range(CONV_WIDTH):
        wb_ref[jtap * SUBLANES:(jtap + 1) * SUBLANES, :] = jnp.broadcast_to(
            w_ref[jtap:jtap + 1, :], (SUBLANES, dc))

    base = CONV_HALO - (CONV_WIDTH - 1)

    taps = {}
    for jtap in range(CONV_WIDTH):
        a, b = divmod(base + jtap, SUBLANES)
        taps.setdefault(b, []).append((a, jtap))

    def conv_body(r, carry):
        r0 = pl.multiple_of(r * CONV_ROWS, CONV_ROWS)
        for lc in range(dc // LANES):
            cols = slice(lc * LANES, (lc + 1) * LANES)
            win = u_ref[pl.ds(r0, CONV_ROWS + CONV_HALO), cols]
            for b, group in taps.items():
                span = CONV_ROWS + SUBLANES * max(a for a, _ in group)
                sh_ref[lc, b, 0:span, :] = win[b:b + span, :]
            acc = jnp.zeros((CONV_ROWS, LANES), F32)
            for b, group in taps.items():
                for a, jtap in group:
                    w_tap = wb_ref[jtap * SUBLANES:(jtap + 1) * SUBLANES, cols]
                    w_rows = jnp.concatenate([w_tap] * (CONV_ROWS // SUBLANES), axis=0)
                    acc = acc + sh_ref[lc, b, SUBLANES * a:SUBLANES * a + CONV_ROWS, :] * w_rows
            cv_ref[pl.ds(r0, CONV_ROWS), cols] = acc + bdw_ref[:, cols]
        return carry

    lax.fori_loop(0, ts // CONV_ROWS, conv_body, 0)

    def norm_body(r, carry):
        r0 = pl.multiple_of(r * NORM_ROWS, NORM_ROWS)
        xr = cv_ref[pl.ds(r0, NORM_ROWS), :]
        mu = jnp.mean(xr, axis=-1, keepdims=True)
        xc = xr - mu
        var = jnp.mean(xc * xc, axis=-1, keepdims=True)
        y = (xc * lax.rsqrt(var + EPS)) * gln_ref[...] + bln_ref[...]
        o_ref[pl.ds(r0, NORM_ROWS), :] = (y * _sigmoid(y)).astype(BF16)
        return carry

    lax.fori_loop(0, ts // NORM_ROWS, norm_body, 0)


def _conv(p, w_dw, b_dw, g_ln, b_ln, later_weights, batch, seq, ts=512):
    m = p.shape[0]
    dc = w_dw.shape[1]
    per_b = seq // ts
    cast_in, cast_out, cast_shapes = _cast_specs(
        later_weights, batch * per_b, lambda b, s: (b * per_b + s, 0))
    return pl.pallas_call(
        _conv_kernel,
        out_shape=[jax.ShapeDtypeStruct((m, dc), BF16)] + cast_shapes,
        grid=(batch, per_b),
        in_specs=[
            pl.BlockSpec((ts, dc), lambda b, s: (b * per_b + s, 0)),
            pl.BlockSpec((ts, dc), lambda b, s: (b * per_b + s, 1)),
            pl.BlockSpec((CONV_WIDTH, dc), lambda b, s: (0, 0)),
            pl.BlockSpec((1, dc), lambda b, s: (0, 0)),
            pl.BlockSpec((1, dc), lambda b, s: (0, 0)),
            pl.BlockSpec((1, dc), lambda b, s: (0, 0)),
        ] + cast_in,
        out_specs=[pl.BlockSpec((ts, dc), lambda b, s: (b * per_b + s, 0))] + cast_out,
        scratch_shapes=[
            pltpu.VMEM((ts + CONV_HALO, dc), F32),
            pltpu.VMEM((ts, dc), F32),
            pltpu.VMEM((CONV_WIDTH * SUBLANES, dc), F32),
            pltpu.VMEM((dc // LANES, SUBLANES, CONV_ROWS + CONV_HALO, LANES), F32),
        ],
        compiler_params=_params(("arbitrary", "arbitrary")),
        name="conv",
    )(p, p, w_dw, b_dw, g_ln, b_ln, *later_weights)


Q_TILE = 256
Q_SUB = 128
Q_SUBS = Q_TILE // Q_SUB
KEY_TILE = 256
KEY_SUB = 256
SEL_ROWS = 64
LANE_CHUNKS = KEY_TILE // LANES
INDEX_BITS = 13


def _fold_lanes(acc, t, op):
    for ch in range(t.shape[1] // LANES):
        acc = op(acc, t[:, ch * LANES:(ch + 1) * LANES])
    return acc


def _attn_kernel(qi_ref, q_ref, k_ref, v_ref, kk_ref, wi_ref, o_ref,
                 qim_ref, qm_ref, wrep_ref, key_ref, bias_ref, m_ref, acc_ref, thr_ref,
                 cut_ref, cand_ref, idx_ref, *, topk):
    i = pl.program_id(1)
    p0 = i * Q_TILE
    n_kt = (p0 + Q_TILE + KEY_TILE - 1) // KEY_TILE
    idx_scale = (IDX_DIM ** -0.5) * (IDX_HEADS ** -0.5)
    attn_scale = HEAD_DIM ** -0.5

    lane = lax.broadcasted_iota(I32, (Q_TILE, LANES), 1)
    lo = lane < HEAD_DIM
    lo_sub = lax.broadcasted_iota(I32, (Q_SUB, LANES), 1) < HEAD_DIM
    row = lax.broadcasted_iota(I32, (Q_TILE, KEY_TILE), 0)
    col = lax.broadcasted_iota(I32, (Q_TILE, KEY_TILE), 1)
    q_chunk = (p0 + row) >> CHUNK_SHIFT

    wi = wi_ref[...]
    for hp in range(N_HEADS // 2):
        cols = slice(hp * LANES, (hp + 1) * LANES)
        q_p = q_ref[:, cols].astype(F32) * (attn_scale * LOG2_E)
        qm_ref[2 * hp] = jnp.where(lo, q_p, 0.0).astype(BF16)
        qm_ref[2 * hp + 1] = jnp.where(lo, 0.0, q_p).astype(BF16)
        for rs in range(Q_SUBS):
            qi_p = qi_ref[rs * Q_SUB:(rs + 1) * Q_SUB, cols]
            zero_i = jnp.zeros_like(qi_p)
            qim_ref[rs, (2 * hp) * Q_SUB:(2 * hp + 1) * Q_SUB, :] = jnp.where(lo_sub, qi_p, zero_i)
            qim_ref[rs, (2 * hp + 1) * Q_SUB:(2 * hp + 2) * Q_SUB, :] = jnp.where(lo_sub, zero_i, qi_p)
    for h in range(IDX_HEADS):
        wrep_ref[h] = jnp.broadcast_to(wi[:, h:h + 1], (Q_TILE, LANES))

    sub_row = lax.broadcasted_iota(I32, (Q_SUB, KEY_SUB), 0)
    sub_col = lax.broadcasted_iota(I32, (Q_SUB, KEY_SUB), 1)

    def score_body(kt, carry):
        k0 = pl.multiple_of(kt * KEY_TILE, KEY_TILE)
        for ks in range(KEY_TILE // KEY_SUB):
            kk_s = kk_ref[pl.ds(k0 + ks * KEY_SUB, KEY_SUB), :]
            for rs in range(Q_SUBS):
                rows = slice(rs * Q_SUB, (rs + 1) * Q_SUB)
                rel = _dot_nt(qim_ref[rs], kk_s)
                acc = [jnp.zeros((Q_SUB, LANES), F32) for _ in range(KEY_SUB // LANES)]
                for h in range(IDX_HEADS):
                    w_h = wrep_ref[h, rows, :]
                    for ch in range(KEY_SUB // LANES):
                        r = rel[h * Q_SUB:(h + 1) * Q_SUB, ch * LANES:(ch + 1) * LANES]
                        acc[ch] = acc[ch] + jnp.maximum(r, 0.0) * w_h
                score = jnp.concatenate(acc, axis=-1) * idx_scale
                adm = ((k0 + ks * KEY_SUB + sub_col) >> CHUNK_SHIFT) <= (
                    (p0 + rs * Q_SUB + sub_row) >> CHUNK_SHIFT)
                key_ref[kt, rows, ks * KEY_SUB:(ks + 1) * KEY_SUB] = jnp.where(adm, score, -jnp.inf)
        return carry

    lax.fori_loop(0, n_kt, score_body, 0)

    kf = float(topk)

    def key_to_float(key_s):
        bits = key_s ^ ((key_s >> 31) & 0x7FFFFFFF)
        return jnp.where(key_s <= NEG_INF_KEY, -jnp.inf, pltpu.bitcast(bits, F32))

    lane_s = lax.broadcasted_iota(I32, (SEL_ROWS, LANES), 1)

    row_groups = [slice(rg * SEL_ROWS, (rg + 1) * SEL_ROWS) for rg in range(Q_TILE // SEL_ROWS)]

    def count(indicator):
        def body(kt, cs):
            out = []
            for rows, c in zip(row_groups, cs):
                for ch in range(LANE_CHUNKS):
                    kc = key_ref[kt, rows, ch * LANES:(ch + 1) * LANES]
                    c = c + indicator(kc, kt * KEY_TILE + ch * LANES + lane_s, rows)
                out.append(c)
            return tuple(out)

        zero = jnp.zeros((SEL_ROWS, LANES), F32)
        cs = lax.fori_loop(0, n_kt, body, tuple(zero for _ in row_groups))
        return jnp.sum(jnp.concatenate(cs, axis=0), axis=-1, keepdims=True)

    def bisect_body(it, prefix):
        bit = lax.shift_left(jnp.int32(1), jnp.int32(31) - it)
        cand_u = prefix | bit
        cand_ref[...] = key_to_float(cand_u ^ INT_MIN)
        cnt = count(lambda kc, _, rows: jnp.where(kc >= cand_ref[rows, :], 1.0, 0.0))
        return jnp.where(cnt >= kf, cand_u, prefix)

    prefix = lax.fori_loop(0, 32, bisect_body, jnp.zeros((Q_TILE, LANES), I32))
    thr = key_to_float(prefix ^ INT_MIN)
    thr_ref[...] = thr

    n_ge = count(lambda kc, _, rows: jnp.where(kc >= thr_ref[rows, :], 1.0, 0.0))
    cut_ref[...] = jnp.full((Q_TILE, LANES), 2 ** INDEX_BITS, I32)
    tied = jnp.where(thr[:, :1] == -jnp.inf, 0.0, jnp.where(n_ge > kf, 1.0, 0.0))
    surplus = jnp.max(tied)

    @pl.when(surplus > 0.0)
    def _():
        n_gt = count(lambda kc, _, rows: jnp.where(kc > thr_ref[rows, :], 1.0, 0.0))
        want = kf - n_gt

        def cut_body(it, cut):
            bit = lax.shift_left(jnp.int32(1), jnp.int32(INDEX_BITS - 1) - it)
            cand = cut | bit
            idx_ref[...] = cand
            cnt = count(lambda kc, idx, rows: jnp.where(
                kc == thr_ref[rows, :], jnp.where(idx < idx_ref[rows, :], 1.0, 0.0), 0.0))
            return jnp.where(cnt <= want, cand, cut)

        cut_ref[...] = lax.fori_loop(0, INDEX_BITS, cut_body, jnp.zeros((Q_TILE, LANES), I32))

    cut = cut_ref[...]

    def bias_body(kt, carry):
        k0 = pl.multiple_of(kt * KEY_TILE, KEY_TILE)
        kt_keys = key_ref[kt]
        adm = ((k0 + col) >> CHUNK_SHIFT) <= q_chunk
        pieces = []
        for ch in range(LANE_CHUNKS):
            kc = kt_keys[:, ch * LANES:(ch + 1) * LANES]
            idx = k0 + ch * LANES + lane
            tie = jnp.where(kc == thr, jnp.where(idx < cut, 0.0, -jnp.inf), -jnp.inf)
            pieces.append(jnp.where(kc > thr, 0.0, tie))
        b = jnp.concatenate(pieces, axis=-1)
        bias_ref[kt] = jnp.where(adm, b, -jnp.inf)
        return carry

    lax.fori_loop(0, n_kt, bias_body, 0)

    for h in range(N_HEADS):
        m_ref[h] = jnp.full((Q_TILE, LANES), NEG_BIG, F32)
        acc_ref[h] = jnp.zeros((Q_TILE, LANES), F32)

    lo_kv = lax.broadcasted_iota(I32, (KEY_TILE, LANES), 1) < HEAD_DIM

    def attn_body(kt, carry):
        k0 = pl.multiple_of(kt * KEY_TILE, KEY_TILE)
        for hp in range(N_HEADS // 2):
            cols = slice(hp * LANES, (hp + 1) * LANES)
            k_t = k_ref[pl.ds(k0, KEY_TILE), cols]
            v_t = v_ref[pl.ds(k0, KEY_TILE), cols]
            one = jnp.ones_like(v_t)
            v_ext = (jnp.where(lo_kv, v_t, one), jnp.where(lo_kv, one, v_t))
            for par in range(2):
                h = 2 * hp + par
                s = _dot_nt(qm_ref[h], k_t) + bias_ref[kt]
                t_max = jnp.max(_fold_lanes(s[:, :LANES], s[:, LANES:], jnp.maximum),
                                axis=-1, keepdims=True)
                m_old = m_ref[h]
                m_new = jnp.maximum(m_old, t_max)
                alpha = jnp.exp2(m_old - m_new)
                e = jnp.concatenate(
                    [jnp.exp2(s[:, ch * LANES:(ch + 1) * LANES] - m_new) for ch in range(LANE_CHUNKS)],
                    axis=-1)
                acc_ref[h] = alpha * acc_ref[h] + _dot(e.astype(BF16), v_ext[par])
                m_ref[h] = m_new
        return carry

    lax.fori_loop(0, n_kt, attn_body, 0)

    for hp in range(N_HEADS // 2):
        acc_e = acc_ref[2 * hp]
        acc_o = acc_ref[2 * hp + 1]
        num = jnp.where(lo, acc_e, acc_o)
        den = pltpu.roll(jnp.where(lo, acc_o, acc_e), HEAD_DIM, 1)
        o_ref[:, hp * LANES:(hp + 1) * LANES] = (num / den).astype(BF16)


def _attn(p, kk, wi, batch, seq, topk):
    m = p.shape[0]
    nq = seq // Q_TILE
    n_kt = seq // KEY_TILE
    once = pl.Buffered(1)
    return pl.pallas_call(
        functools.partial(_attn_kernel, topk=topk),
        out_shape=jax.ShapeDtypeStruct((m, D_ATTN), BF16),
        grid=(batch, nq),
        in_specs=[
            pl.BlockSpec((Q_TILE, D_ATTN), lambda b, i: (b * nq + i, 5)),
            pl.BlockSpec((Q_TILE, D_ATTN), lambda b, i: (b * nq + i, 2)),
            pl.BlockSpec((seq, D_ATTN), lambda b, i: (b, 3), pipeline_mode=once),
            pl.BlockSpec((seq, D_ATTN), lambda b, i: (b, 4), pipeline_mode=once),
            pl.BlockSpec((seq, LANES), lambda b, i: (b, 0), pipeline_mode=once),
            pl.BlockSpec((Q_TILE, LANES), lambda b, i: (b * nq + i, 0)),
        ],
        out_specs=pl.BlockSpec((Q_TILE, D_ATTN), lambda b, i: (b * nq + i, 0)),
        scratch_shapes=[
            pltpu.VMEM((Q_SUBS, IDX_HEADS * Q_SUB, LANES), BF16),
            pltpu.VMEM((N_HEADS, Q_TILE, LANES), BF16),
            pltpu.VMEM((IDX_HEADS, Q_TILE, LANES), F32),
            pltpu.VMEM((n_kt, Q_TILE, KEY_TILE), F32),
            pltpu.VMEM((n_kt, Q_TILE, KEY_TILE), F32),
            pltpu.VMEM((N_HEADS, Q_TILE, LANES), F32),
            pltpu.VMEM((N_HEADS, Q_TILE, LANES), F32),
            pltpu.VMEM((Q_TILE, LANES), F32),
            pltpu.VMEM((Q_TILE, LANES), I32),
            pltpu.VMEM((Q_TILE, LANES), F32),
            pltpu.VMEM((Q_TILE, LANES), I32),
        ],
        compiler_params=_params(("arbitrary", "arbitrary")),
        name="attn",
    )(p, p, p, p, kk, wi)


MERGE_ROWS = 256


def _rms_rows(x, g):
    ms = jnp.mean(x * x, axis=-1, keepdims=True)
    return (x * lax.rsqrt(ms + EPS)) * g


def _merge_kernel(uc_ref, oa_ref, gc_ref, ga_ref, x_ref, gt_ref, gp_ref, g2_ref, sc2_ref, sh2_ref,
                  wc_ref, wa_ref, wo_ref, *rest):
    n_cast = (len(rest) - 2) // 2
    o_ref, h2_ref = rest[n_cast], rest[n_cast + 1]
    _cast_slabs(rest[:n_cast] + rest[n_cast + 2:])

    tm = x_ref.shape[0]
    gt = gt_ref[0]
    gp = gp_ref[...]
    g2 = g2_ref[...]
    sc2 = sc2_ref[0]
    sh2 = sh2_ref[0]

    def body(r, carry):
        r0 = pl.multiple_of(r * MERGE_ROWS, MERGE_ROWS)
        rows = pl.ds(r0, MERGE_ROWS)
        yc = _dot(uc_ref[rows, :], wc_ref[...])
        ya = _dot(oa_ref[rows, :], wa_ref[...])
        gc = _sigmoid(gc_ref[rows, :].astype(F32))
        ga = _sigmoid(ga_ref[rows, :].astype(F32))
        merged = (gc * yc + ga * ya).astype(BF16)
        mixed = _dot(merged, wo_ref[...])
        x1 = x_ref[rows, :] + gt * _rms_rows(mixed, gp)
        o_ref[rows, :] = x1
        h2_ref[rows, :] = _adaln_rows(x1, g2, sc2, sh2).astype(BF16)
        return carry

    lax.fori_loop(0, tm // MERGE_ROWS, body, 0)


def _cast_specs(weights, steps, index_map):
    in_specs, out_specs, out_shapes = [], [], []
    for w in weights:
        rows, cols = w.shape
        slab = rows // steps
        assert slab * steps == rows and slab % (2 * SUBLANES) == 0, (w.shape, steps)
        in_specs.append(pl.BlockSpec((slab, cols), index_map))
        out_specs.append(pl.BlockSpec((slab, cols), index_map))
        out_shapes.append(jax.ShapeDtypeStruct((rows, cols), BF16))
    return in_specs, out_specs, out_shapes


def _cast_slabs(refs):
    half = len(refs) // 2
    for src, dst in zip(refs[:half], refs[half:]):
        dst[...] = src[...].astype(BF16)


def _merge(uc, oa, p, x2, gt1, g_post, g_pre_next, sc_next, sh_next, wc, wa, wo, next_weights,
           seq, tm=256):
    m, d = x2.shape
    dc = uc.shape[1]
    per_b = seq // tm
    gate_blk = p.shape[1] // d - 2
    once = pl.Buffered(1)
    cast_in, cast_out, cast_shapes = _cast_specs(next_weights, m // tm, lambda i: (i, 0))
    return pl.pallas_call(
        _merge_kernel,
        out_shape=[jax.ShapeDtypeStruct((m, d), F32), jax.ShapeDtypeStruct((m, d), BF16)]
        + cast_shapes,
        grid=(m // tm,),
        in_specs=[
            pl.BlockSpec((tm, dc), lambda i: (i, 0)),
            pl.BlockSpec((tm, dc), lambda i: (i, 0)),
            pl.BlockSpec((tm, d), lambda i: (i, gate_blk)),
            pl.BlockSpec((tm, d), lambda i: (i, gate_blk + 1)),
            pl.BlockSpec((tm, d), lambda i: (i, 0)),
            pl.BlockSpec((1, 1, d), lambda i: (i // per_b, 0, 0)),
            pl.BlockSpec((1, d), lambda i: (0, 0)),
            pl.BlockSpec((1, d), lambda i: (0, 0)),
            pl.BlockSpec((1, 1, d), lambda i: (i // per_b, 0, 0)),
            pl.BlockSpec((1, 1, d), lambda i: (i // per_b, 0, 0)),
            pl.BlockSpec((dc, d), lambda i: (0, 0), pipeline_mode=once),
            pl.BlockSpec((dc, d), lambda i: (0, 0), pipeline_mode=once),
            pl.BlockSpec((d, d), lambda i: (0, 0), pipeline_mode=once),
        ] + cast_in,
        out_specs=[pl.BlockSpec((tm, d), lambda i: (i, 0)), pl.BlockSpec((tm, d), lambda i: (i, 0))]
        + cast_out,
        compiler_params=_params(("arbitrary",)),
        name="merge",
    )(uc, oa, p, p, x2, gt1, g_post, g_pre_next, sc_next, sh_next, wc, wa, wo, *next_weights)


FFN_ROWS = 512


def _ffn_kernel(x_ref, h_ref, gt_ref, gp_ref, wg_ref, wu_ref, wd_ref, o_ref, acc_ref):
    j = pl.program_id(1)
    tm = x_ref.shape[0]
    n_rows = tm // FFN_ROWS

    @pl.when(j == 0)
    def _():
        acc_ref[...] = jnp.zeros(acc_ref.shape, F32)

    for r in range(n_rows):
        rows = slice(r * FFN_ROWS, (r + 1) * FFN_ROWS)
        h = h_ref[rows, :]
        a = _dot(h, wg_ref[...])
        b = _dot(h, wu_ref[...])
        act = ((a * _sigmoid(a)) * b).astype(BF16)
        acc_ref[rows, :] = acc_ref[rows, :] + _dot(act, wd_ref[...])

    @pl.when(j == pl.num_programs(1) - 1)
    def _():
        gt = gt_ref[0]
        gp = gp_ref[...]

        def body(r, carry):
            rows = pl.ds(pl.multiple_of(r * ROW_CHUNK, ROW_CHUNK), ROW_CHUNK)
            o_ref[rows, :] = x_ref[rows, :] + gt * _rms_rows(acc_ref[rows, :], gp)
            return carry

        lax.fori_loop(0, tm // ROW_CHUNK, body, 0)


def _ffn(x2, h2, gt2, g_post, wg, wu, wd, seq, tm=1024, tf=512):
    m, d = x2.shape
    dff = wg.shape[1]
    per_b = seq // tm
    once = pl.Buffered(1)
    return pl.pallas_call(
        _ffn_kernel,
        out_shape=jax.ShapeDtypeStruct((m, d), F32),
        grid=(m // tm, dff // tf),
        in_specs=[
            pl.BlockSpec((tm, d), lambda i, j: (i, 0), pipeline_mode=once),
            pl.BlockSpec((tm, d), lambda i, j: (i, 0), pipeline_mode=once),
            pl.BlockSpec((1, 1, d), lambda i, j: (i // per_b, 0, 0)),
            pl.BlockSpec((1, d), lambda i, j: (0, 0)),
            pl.BlockSpec((d, tf), lambda i, j: (0, j)),
            pl.BlockSpec((d, tf), lambda i, j: (0, j)),
            pl.BlockSpec((tf, d), lambda i, j: (j, 0)),
        ],
        out_specs=pl.BlockSpec((tm, d), lambda i, j: (i, 0)),
        scratch_shapes=[pltpu.VMEM((tm, d), F32)],
        compiler_params=_params(("arbitrary", "arbitrary")),
        name="ffn",
    )(x2, h2, gt2, g_post, wg, wu, wd)


def kernel(x, c, w_ada, b_ada, g_pre_mix, w_in, w_dw, b_dw, g_conv_ln, b_conv_ln, w_conv_out, w_attn_out, w_o, g_post_mix, g_pre_ffn, w_gate, w_up, w_down, g_post_ffn):
    batch, seq, d = x.shape
    depth = w_ada.shape[0]
    topk = min(TOPK_MAX, seq // 4)

    off_glu = 2 * D_CONV
    off_ki = off_glu + 3 * D_ATTN + IDX_HEADS * IDX_DIM
    off_wi = off_ki + IDX_DIM
    off_gate = off_wi + IDX_HEADS

    c8 = jnp.zeros((SUBLANES, d), F32).at[:batch].set(c)
    x2 = x.reshape(batch * seq, d)
    for l in range(depth):
        mod = _ada(c8, w_ada[l], b_ada[l][None, :])[:batch]
        sh1, sc1, gt1, sh2, sc2, gt2 = [t[:, None, :] for t in jnp.split(mod, 6, axis=-1)]

        w = w_in[l]
        w_bf = w.astype(BF16)
        w_gates = w_bf[:, off_gate:]
        w_ki = w_bf[:, off_ki:off_wi]
        w_wi = jnp.pad(w_bf[:, off_wi:off_gate], ((0, 0), (0, LANES - IDX_HEADS)))
        w_small = jnp.concatenate([w_ki, w_ki, w_wi], axis=1)

        p, kk, wi = _proj(x2, g_pre_mix[l][None, :], sc1, sh1, w_bf, w_gates, w_small, seq,
                          n_main_cols=off_ki)
        uc, wc_bf, wa_bf, wo_bf = _conv(
            p, w_dw[l], b_dw[l][None, :], g_conv_ln[l][None, :], b_conv_ln[l][None, :],
            (w_conv_out[l], w_attn_out[l], w_o[l]), batch, seq)
        oa = _attn(p, kk, wi, batch, seq, topk)
        x2, h2, wg_bf, wu_bf, wd_bf = _merge(
            uc, oa, p, x2, gt1, g_post_mix[l][None, :], g_pre_ffn[l][None, :], sc2, sh2,
            wc_bf, wa_bf, wo_bf, (w_gate[l], w_up[l], w_down[l]), seq)
        x2 = _ffn(x2, h2, gt2, g_post_ffn[l][None, :], wg_bf, wu_bf, wd_bf, seq)
    return x2.reshape(batch, seq, d)
```
